```python
import math
import jax, jax.numpy as jnp
from jax import lax
import numpy as np

D_MODEL = 1024
BATCH = 8
SEQ = 4096
DEPTH = 4

FOX_HEADS = 8
FOX_HEAD_DIM = 64
FOX_WIDTH = FOX_HEADS * FOX_HEAD_DIM
Q_BLOCK = 128
SSD_HEADS = 8
SSD_HEAD_DIM = 64
SSD_WIDTH = SSD_HEADS * SSD_HEAD_DIM
SSD_GROUPS = 2
SSD_STATE = 64
SSD_CONV = 4
SSD_CHUNK = 128
SSD_CONV_DIM = SSD_WIDTH + 2 * SSD_GROUPS * SSD_STATE
S5_GROUP = 16
S5_WIDTH = 512
S5_GROUPS = S5_WIDTH // S5_GROUP
S5_STATE = 64
D_FF = 2816
FFN_CONV = 3
N_BRANCH = 3
BRANCH_WIDTH = 512
ALPHA = (2 * DEPTH) ** 0.25
BETA = (8 * DEPTH) ** -0.25
LN_EPS = 1e-5
RMS_EPS = 1e-5
D_IN_PROJ = 3 * FOX_WIDTH + FOX_HEADS + SSD_WIDTH + SSD_CONV_DIM + SSD_HEADS + S5_WIDTH + N_BRANCH * D_MODEL

kernel_name = "fox_ssd_s5_gated_hybrid_deepnorm"


def _split_points():
    sizes = [FOX_WIDTH, FOX_WIDTH, FOX_WIDTH, FOX_HEADS, SSD_WIDTH, SSD_CONV_DIM,
             SSD_HEADS, S5_WIDTH]
    pts, acc = [], 0
    for s in sizes:
        acc += s
        pts.append(acc)
    return pts


def layer_norm(x, g, b):
    xf = x.astype(jnp.float32)
    mu = jnp.mean(xf, axis=-1, keepdims=True)
    var = jnp.mean(jnp.square(xf - mu), axis=-1, keepdims=True)
    return ((xf - mu) * lax.rsqrt(var + LN_EPS) * g + b).astype(x.dtype)


def causal_dwconv(x, w, b):
    k, c = w.shape
    y = lax.conv_general_dilated(
        x, w[:, None, :].astype(x.dtype), window_strides=(1,), padding=((k - 1, 0),),
        dimension_numbers=("NWC", "WIO", "NWC"), feature_group_count=c)
    return y + b.astype(x.dtype)


def fox_attention(q, k, v, f_logit):
    b, l, h, dh = q.shape
    cum = jnp.cumsum(jax.nn.log_sigmoid(f_logit.astype(jnp.float32)), axis=1)
    cum = cum.transpose(0, 2, 1)
    scale = dh ** -0.5
    outs = []
    for i in range(l // Q_BLOCK):
        q0, q1 = i * Q_BLOCK, (i + 1) * Q_BLOCK
        s = jnp.einsum('bqhd,bkhd->bhqk', q[:, q0:q1], k[:, :q1]).astype(jnp.float32) * scale
        s = s + cum[:, :, q0:q1, None] - cum[:, :, None, :q1]
        causal = jnp.arange(q1)[None, :] <= jnp.arange(q0, q1)[:, None]
        p = jax.nn.softmax(jnp.where(causal, s, -jnp.inf), axis=-1)
        outs.append(jnp.einsum('bhqk,bkhd->bqhd', p.astype(v.dtype), v[:, :q1]))
    return jnp.concatenate(outs, axis=1).reshape(b, l, h * dh)


def segsum(a):
    t = a.shape[-1]
    cs = jnp.cumsum(a, axis=-1)
    diff = cs[..., :, None] - cs[..., None, :]
    return jnp.where(jnp.tril(jnp.ones((t, t), dtype=bool)), diff, -jnp.inf)


def ssd_mixer(z, xbc, dt_raw, conv_w, conv_b, dt_bias, a_log, d_skip, norm_w):
    f32 = jnp.float32
    bsz, l, _ = z.shape
    g, e, n, p, q = SSD_GROUPS, SSD_HEADS // SSD_GROUPS, SSD_STATE, SSD_HEAD_DIM, SSD_CHUNK
    c = l // q
    xbc = jax.nn.silu(causal_dwconv(xbc, conv_w, conv_b))
    xs, bm, cm = jnp.split(xbc, [SSD_WIDTH, SSD_WIDTH + g * n], axis=-1)
    xh = xs.reshape(bsz, l, SSD_HEADS, p).astype(f32)
    dt = jax.nn.softplus(dt_raw.astype(f32) + dt_bias.astype(f32))
    a = -jnp.exp(a_log.astype(f32))
    xdt = (xh * dt[..., None]).reshape(bsz, c, q, g, e, p)
    adt = (dt * a).reshape(bsz, c, q, g, e).transpose(0, 3, 4, 1, 2)
    bc = bm.astype(f32).reshape(bsz, c, q, g, n)
    cc = cm.astype(f32).reshape(bsz, c, q, g, n)
    a_cs = jnp.cumsum(adt, axis=-1)
    decay_in = jnp.exp(segsum(adt))
    cb = jnp.einsum('bclgn,bcsgn->bgcls', cc, bc)
    y_diag = jnp.einsum('bgecls,bcsgep->bclgep', cb[:, :, None] * decay_in, xdt)
    decay_to_end = jnp.exp(a_cs[..., -1:] - a_cs).transpose(0, 3, 4, 1, 2)
    states = jnp.einsum('bclgn,bclgep->bcgepn', bc, xdt * decay_to_end[..., None])
    states = jnp.concatenate([jnp.zeros_like(states[:, :1]), states], axis=1)
    chunk_tot = jnp.pad(a_cs[..., -1], ((0, 0), (0, 0), (0, 0), (1, 0)))
    chunk_decay = jnp.exp(segsum(chunk_tot))
    states = jnp.einsum('bgezc,bcgepn->bzgepn', chunk_decay, states)[:, :-1]
    decay_out = jnp.exp(a_cs).transpose(0, 3, 4, 1, 2)[..., None]
    y_off = jnp.einsum('bclgn,bcgepn->bclgep', cc, states) * decay_out
    y = (y_diag + y_off).reshape(bsz, l, SSD_HEADS, p) + d_skip.astype(f32)[:, None] * xh
    y = y.reshape(bsz, l, SSD_WIDTH) * jax.nn.silu(z.astype(f32))
    yg = y.reshape(bsz, l, g, SSD_WIDTH // g)
    yg = yg * lax.rsqrt(jnp.mean(yg * yg, axis=-1, keepdims=True) + RMS_EPS)
    return (yg.reshape(bsz, l, SSD_WIDTH) * norm_w.astype(f32)).astype(z.dtype)


def _complex_linear_combine(earlier, later):
    ar1, ai1, br1, bi1 = earlier
    ar2, ai2, br2, bi2 = later
    return (ar2 * ar1 - ai2 * ai1,
            ar2 * ai1 + ai2 * ar1,
            ar2 * br1 - ai2 * bi1 + br2,
            ar2 * bi1 + ai2 * br1 + bi2)


def s5_mixer(u, a_re, a_im, b_re, b_im, c_re, c_im, log_step, d_skip, w_glu, b_glu):
    f32 = jnp.float32
    bsz, l, _ = u.shape
    lam_re = jnp.minimum(a_re.astype(f32), -1e-4)
    lam_im = a_im.astype(f32)
    step = jnp.exp(log_step.astype(f32))[:, None]
    mag = jnp.exp(lam_re * step)
    abar_re = mag * jnp.cos(lam_im * step)
    abar_im = mag * jnp.sin(lam_im * step)
    den = lam_re * lam_re + lam_im * lam_im
    num_re = abar_re - 1.0
    k_re = (num_re * lam_re + abar_im * lam_im) / den
    k_im = (abar_im * lam_re - num_re * lam_im) / den
    bre, bim = b_re.astype(f32), b_im.astype(f32)
    bb_re = k_re[..., None] * bre - k_im[..., None] * bim
    bb_im = k_re[..., None] * bim + k_im[..., None] * bre
    ug = u.astype(f32).reshape(bsz, l, S5_GROUPS, S5_GROUP)
    bu_re = jnp.einsum('blgh,gnh->lbgn', ug, bb_re)
    bu_im = jnp.einsum('blgh,gnh->lbgn', ug, bb_im)
    a_seq_re = jnp.broadcast_to(abar_re[None, None], (l, 1) + abar_re.shape)
    a_seq_im = jnp.broadcast_to(abar_im[None, None], (l, 1) + abar_im.shape)
    _, _, h_re, h_im = lax.associative_scan(
        _complex_linear_combine, (a_seq_re, a_seq_im, bu_re, bu_im), axis=0)
    y = (jnp.einsum('lbgn,ghn->blgh', h_re, c_re.astype(f32))
         - jnp.einsum('lbgn,ghn->blgh', h_im, c_im.astype(f32)))
    y = y.reshape(bsz, l, S5_WIDTH) + d_skip.astype(f32) * u.astype(f32)
    gy = jax.nn.gelu(y)
    out = gy * jax.nn.sigmoid(gy @ w_glu.astype(f32) + b_glu.astype(f32))
    return out.astype(u.dtype)


def mixer_sublayer(x, w_in, fox_f_bias, ssd_conv_w, ssd_conv_b, ssd_dt_bias, ssd_a_log,
                   ssd_d, ssd_norm_w, s5_a_re, s5_a_im, s5_b_re, s5_b_im, s5_c_re, s5_c_im,
                   s5_log_step, s5_d, s5_w_glu, s5_b_glu, w_branch, b_gate, w_out):
    bsz, l, _ = x.shape
    proj = x @ w_in
    q, k, v, f_logit, z, xbc, dt_raw, u, gate_logit = jnp.split(proj, _split_points(), axis=-1)
    hs = (bsz, l, FOX_HEADS, FOX_HEAD_DIM)
    y_a = fox_attention(q.reshape(hs), k.reshape(hs), v.reshape(hs), f_logit + fox_f_bias)
    y_b = ssd_mixer(z, xbc, dt_raw, ssd_conv_w, ssd_conv_b, ssd_dt_bias, ssd_a_log,
                    ssd_d, ssd_norm_w)
    y_c = s5_mixer(u, s5_a_re, s5_a_im, s5_b_re, s5_b_im, s5_c_re, s5_c_im,
                   s5_log_step, s5_d, s5_w_glu, s5_b_glu)
    branches = jnp.stack([y_a, y_b.astype(y_a.dtype), y_c.astype(y_a.dtype)], axis=2)
    proj_br = jnp.einsum('blrw,rwd->blrd', branches, w_branch)
    gates = jax.nn.sigmoid(gate_logit.reshape(bsz, l, N_BRANCH, D_MODEL) + b_gate)
    merged = jnp.sum(gates * proj_br, axis=2)
    return merged @ w_out


def conv_ffn(x, w_up, conv_w, conv_b, w_down):
    h = causal_dwconv(x @ w_up, conv_w, conv_b)
    val, gate = jnp.split(h, 2, axis=-1)
    return (jax.nn.silu(gate) * val) @ w_down


def setup_inputs(seed: int = 0) -> dict:
    key = jax.random.key(seed)
    ks = iter(jax.random.split(key, 40))
    nrm = lambda shape, s: jax.random.normal(next(ks), shape, jnp.float32) * s
    L_ = DEPTH
    dt0 = jnp.exp(jax.random.uniform(next(ks), (L_, SSD_HEADS), jnp.float32,
                                     math.log(1e-3), math.log(1e-1)))
    inputs = {
        "x": nrm((BATCH, SEQ, D_MODEL), 1.0),
        "w_in": nrm((L_, D_MODEL, D_IN_PROJ), D_MODEL ** -0.5),
        "fox_f_bias": 2.0 + nrm((L_, FOX_HEADS), 0.5),
        "ssd_conv_w": nrm((L_, SSD_CONV, SSD_CONV_DIM), SSD_CONV ** -0.5),
        "ssd_conv_b": nrm((L_, SSD_CONV_DIM), 0.01),
        "ssd_dt_bias": dt0 + jnp.log(-jnp.expm1(-dt0)),
        "ssd_a_log": jnp.log(jax.random.uniform(next(ks), (L_, SSD_HEADS), jnp.float32, 1.0, 16.0)),
        "ssd_d": 1.0 + nrm((L_, SSD_HEADS), 0.1),
        "ssd_norm_w": 1.0 + nrm((L_, SSD_WIDTH), 0.02),
        "s5_a_re": -0.5 + nrm((L_, S5_GROUPS, S5_STATE), 0.01),
        "s5_a_im": jnp.pi * jnp.arange(S5_STATE, dtype=jnp.float32) + nrm((L_, S5_GROUPS, S5_STATE), 0.01),
        "s5_b_re": nrm((L_, S5_GROUPS, S5_STATE, S5_GROUP), (2 * S5_GROUP) ** -0.5),
        "s5_b_im": nrm((L_, S5_GROUPS, S5_STATE, S5_GROUP), (2 * S5_GROUP) ** -0.5),
        "s5_c_re": nrm((L_, S5_GROUPS, S5_GROUP, S5_STATE), 0.5),
        "s5_c_im": nrm((L_, S5_GROUPS, S5_GROUP, S5_STATE), 0.5),
        "s5_log_step": jax.random.uniform(next(ks), (L_, S5_GROUPS), jnp.float32,
                                          math.log(1e-3), math.log(1e-1)),
        "s5_d": nrm((L_, S5_WIDTH), 1.0),
        "s5_w_glu": nrm((L_, S5_WIDTH, S5_WIDTH), S5_WIDTH ** -0.5),
        "s5_b_glu": nrm((L_, S5_WIDTH), 0.01),
        "w_branch": nrm((L_, N_BRANCH, BRANCH_WIDTH, D_MODEL), BRANCH_WIDTH ** -0.5),
        "b_gate": nrm((L_, N_BRANCH, D_MODEL), 0.01),
        "w_out": nrm((L_, D_MODEL, D_MODEL), BETA * D_MODEL ** -0.5),
        "ln1_g": 1.0 + nrm((L_, D_MODEL), 0.02),
        "ln1_b": nrm((L_, D_MODEL), 0.01),
        "ffn_w_up": nrm((L_, D_MODEL, 2 * D_FF), D_MODEL ** -0.5),
        "ffn_conv_w": nrm((L_, FFN_CONV, 2 * D_FF), FFN_CONV ** -0.5),
        "ffn_conv_b": nrm((L_, 2 * D_FF), 0.01),
        "ffn_w_down": nrm((L_, D_FF, D_MODEL), BETA * D_FF ** -0.5),
        "ln2_g": 1.0 + nrm((L_, D_MODEL), 0.02),
        "ln2_b": nrm((L_, D_MODEL), 0.01),
    }
    return inputs


def reference(x, w_in, fox_f_bias, ssd_conv_w, ssd_conv_b, ssd_dt_bias, ssd_a_log, ssd_d,
              ssd_norm_w, s5_a_re, s5_a_im, s5_b_re, s5_b_im, s5_c_re, s5_c_im, s5_log_step,
              s5_d, s5_w_glu, s5_b_glu, w_branch, b_gate, w_out, ln1_g, ln1_b, ffn_w_up,
              ffn_conv_w, ffn_conv_b, ffn_w_down, ln2_g, ln2_b):
    for i in range(DEPTH):
        mix = mixer_sublayer(x, w_in[i], fox_f_bias[i], ssd_conv_w[i], ssd_conv_b[i],
                             ssd_dt_bias[i], ssd_a_log[i], ssd_d[i], ssd_norm_w[i],
                             s5_a_re[i], s5_a_im[i], s5_b_re[i], s5_b_im[i], s5_c_re[i],
                             s5_c_im[i], s5_log_step[i], s5_d[i], s5_w_glu[i], s5_b_glu[i],
                             w_branch[i], b_gate[i], w_out[i])
        x = layer_norm(ALPHA * x + mix.astype(x.dtype), ln1_g[i], ln1_b[i])
        ffn = conv_ffn(x, ffn_w_up[i], ffn_conv_w[i], ffn_conv_b[i], ffn_w_down[i])
        x = layer_norm(ALPHA * x + ffn.astype(x.dtype), ln2_g[i], ln2_b[i])
    return x
```

```python
import functools
import math

import jax
import jax.numpy as jnp
from jax import lax
from jax.experimental import pallas as pl
from jax.experimental.pallas import tpu as pltpu

F32 = jnp.float32
BF16 = jnp.bfloat16

D_MODEL = 1024
N_HEADS = 8
HEAD_DIM = 64
WIDTH = N_HEADS * HEAD_DIM
SSD_STATE = 64
SSD_GROUPS = 2
SSD_CONV = 4
SSD_CHUNK = 128
SSD_CONV_DIM = WIDTH + 2 * SSD_GROUPS * SSD_STATE
S5_GROUP = 16
S5_GROUPS = WIDTH // S5_GROUP
S5_STATE = 64
S5_LANES = S5_GROUPS * S5_STATE
D_FF = 2816
FFN_CONV = 3
N_BRANCH = 3
LN_EPS = 1e-5
RMS_EPS = 1e-5
LANE = 128
SUBLANE = 8
NEG_BIG = -1e30
VMEM_LIMIT = 56 * 1024 * 1024

_C_QKV = 0
_C_XBC = _C_QKV + 3 * WIDTH
_C_Z = _C_XBC + SSD_CONV_DIM
_C_DT = _C_Z + WIDTH
_C_U = _C_DT + WIDTH
_C_GATE = _C_U + WIDTH
_C_FL = _C_GATE + N_BRANCH * D_MODEL
_C_END = _C_FL + LANE


def _params(sem):
    return pltpu.CompilerParams(dimension_semantics=sem, vmem_limit_bytes=VMEM_LIMIT)


def _full(shape):
    n = len(shape)
    return pl.BlockSpec(shape, lambda *_: (0,) * n)


def _sigmoid(x):
    return 1.0 / (1.0 + jnp.exp(-x))


def _softplus(x):
    return jnp.maximum(x, 0.0) + jnp.log1p(jnp.exp(-jnp.abs(x)))


def _cumsum_rows(x):
    n = x.shape[0]
    row = lax.broadcasted_iota(jnp.int32, x.shape, 0)
    s = 1
    while s < n:
        x = x + jnp.where(row >= s, pltpu.roll(x, s, 0), 0.0)
        s *= 2
    return x


def _layer_norm(v, g, b):
    mu = jnp.mean(v, axis=-1, keepdims=True)
    d = v - mu
    var = jnp.mean(d * d, axis=-1, keepdims=True)
    return d * lax.rsqrt(var + LN_EPS) * g + b


def _in_proj_kernel(x_ref, w_ref, qkv_ref, xbc_ref, z_ref, dt_ref, u_ref, gate_ref, fl_ref):
    xb = x_ref[...].astype(BF16)

    def mm(c0, c1):
        return jnp.dot(xb, w_ref[:, c0:c1], preferred_element_type=F32)

    qkv_ref[...] = mm(_C_QKV, _C_XBC).astype(BF16)
    xbc_ref[...] = mm(_C_XBC, _C_Z)
    z_ref[...] = mm(_C_Z, _C_DT)
    dt_ref[...] = mm(_C_DT, _C_U)
    u_ref[...] = mm(_C_U, _C_GATE)
    gate_ref[...] = mm(_C_GATE, _C_FL)
    fl_ref[...] = mm(_C_FL, _C_END)


def _in_proj(x2, w, bsz, seq, tm=256):
    m = x2.shape[0]
    nt = seq // tm
    row = lambda i: (i, 0)
    out_shape = (
        jax.ShapeDtypeStruct((m, 3 * WIDTH), BF16),
        jax.ShapeDtypeStruct((m, SSD_CONV_DIM), F32),
        jax.ShapeDtypeStruct((m, WIDTH), F32),
        jax.ShapeDtypeStruct((m, WIDTH), F32),
        jax.ShapeDtypeStruct((seq, bsz * WIDTH), F32),
        jax.ShapeDtypeStruct((m, N_BRANCH * D_MODEL), F32),
        jax.ShapeDtypeStruct((m, LANE), F32),
    )
    out_specs = (
        pl.BlockSpec((tm, 3 * WIDTH), row),
        pl.BlockSpec((tm, SSD_CONV_DIM), row),
        pl.BlockSpec((tm, WIDTH), row),
        pl.BlockSpec((tm, WIDTH), row),
        pl.BlockSpec((tm, WIDTH), lambda i: (i % nt, i // nt)),
        pl.BlockSpec((tm, N_BRANCH * D_MODEL), row),
        pl.BlockSpec((tm, LANE), row),
    )
    return pl.pallas_call(
        _in_proj_kernel,
        grid=(m // tm,),
        in_specs=[pl.BlockSpec((tm, D_MODEL), row), _full(w.shape)],
        out_specs=out_specs,
        out_shape=out_shape,
        compiler_params=_params(("parallel",)),
        name="in_proj",
    )(x2, w)


def _fox_cum_kernel(fl_ref, bias_ref, c_ref):
    v = fl_ref[...] + bias_ref[...]
    ls = jnp.minimum(v, 0.0) - jnp.log1p(jnp.exp(-jnp.abs(v)))
    c_ref[...] = _cumsum_rows(ls)


def _fox_cum(fl, bias_pad, bsz, seq):
    return pl.pallas_call(
        _fox_cum_kernel,
        grid=(bsz,),
        in_specs=[pl.BlockSpec((seq, LANE), lambda b: (b, 0)), _full((1, LANE))],
        out_specs=pl.BlockSpec((seq, LANE), lambda b: (b, 0)),
        out_shape=jax.ShapeDtypeStruct(fl.shape, F32),
        compiler_params=_params(("parallel",)),
        name="fox_cum",
    )(fl, bias_pad)


def _fox_attn_kernel(q_ref, k_ref, v_ref, cq_ref, ck_ref, o_ref, *, tq):
    qi = pl.program_id(1)
    lo = lax.broadcasted_iota(jnp.int32, (1, LANE), 1) < HEAD_DIM
    causal = (lax.broadcasted_iota(jnp.int32, (tq, tq), 0)
              >= lax.broadcasted_iota(jnp.int32, (tq, tq), 1))
    nt = (((1,), (1,)), ((), ()))

    for j in range(N_HEADS // 2):
        c0, c1 = j * LANE, (j + 1) * LANE
        q2 = q_ref[:, c0:c1]
        zero = jnp.zeros_like(q2)
        qm = (jnp.where(lo, q2, zero), jnp.where(lo, zero, q2))
        cq = tuple(cq_ref[:, 2 * j + e:2 * j + e + 1] for e in (0, 1))

        def step(ki, carry, masked, c0=c0, c1=c1, qm=qm, cq=cq, j=j):
            r0 = pl.multiple_of(ki * tq, tq)
            k2 = k_ref[pl.ds(r0, tq), c0:c1]
            v2 = v_ref[pl.ds(r0, tq), c0:c1]
            ck8 = ck_ref[ki]
            new = []
            for e in (0, 1):
                m, l, acc = carry[e]
                s = lax.dot_general(qm[e], k2, nt, preferred_element_type=F32)
                s = s + (cq[e] - ck8[2 * j + e:2 * j + e + 1, :])
                if masked:
                    s = jnp.where(causal, s, NEG_BIG)
                m_new = jnp.maximum(m, jnp.max(s, axis=-1, keepdims=True))
                p = jnp.exp(s - m_new)
                alpha = jnp.exp(m - m_new)
                l = alpha * l + jnp.sum(p, axis=-1, keepdims=True)
                acc = alpha * acc + jnp.dot(p.astype(BF16), v2, preferred_element_type=F32)
                new.append((m_new, l, acc))
            return tuple(new)

        init1 = (jnp.full((tq, 1), NEG_BIG, F32), jnp.zeros((tq, 1), F32),
                 jnp.zeros((tq, LANE), F32))
        carry = lax.fori_loop(0, qi, functools.partial(step, masked=False), (init1, init1))
        carry = step(qi, carry, True)
        (_, l0, a0), (_, l1, a1) = carry
        o_ref[:, c0:c1] = jnp.where(lo, a0 / l0, a1 / l1).astype(o_ref.dtype)


def _fox_attn(qkv, c, ck, bsz, seq, tq=256):
    m = qkv.shape[0]
    nq = seq // tq
    kernel = functools.partial(_fox_attn_kernel, tq=tq)
    return pl.pallas_call(
        kernel,
        grid=(bsz, nq),
        in_specs=[
            pl.BlockSpec((tq, WIDTH), lambda b, i: (b * nq + i, 0)),
            pl.BlockSpec((seq, WIDTH), lambda b, i: (b, 1)),
            pl.BlockSpec((seq, WIDTH), lambda b, i: (b, 2)),
            pl.BlockSpec((tq, LANE), lambda b, i: (b * nq + i, 0)),
            pl.BlockSpec((None, nq, SUBLANE, tq), lambda b, i: (b, 0, 0, 0)),
        ],
        out_specs=pl.BlockSpec((tq, WIDTH), lambda b, i: (b * nq + i, 0)),
        out_shape=jax.ShapeDtypeStruct((m, WIDTH), BF16),
        compiler_params=_params(("parallel", "arbitrary")),
        name="fox_attn",
    )(qkv, qkv, qkv, c, ck)


def _ssd_kernel(xbc_ref, halo_ref, z_ref, dt_ref, cw_ref, cb_ref, dtb_ref, alog_ref, d_ref,
                nw_ref, o_ref, state_ref):
    c = pl.program_id(1)
    q = SSD_CHUNK

    @pl.when(c == 0)
    def _():
        state_ref[...] = jnp.zeros_like(state_ref)

    halo = jnp.where(c > 0, halo_ref[...], 0.0)
    full = jnp.concatenate([halo, xbc_ref[...]], axis=0)
    cw = cw_ref[...]
    acc = cb_ref[...]
    for k in range(SSD_CONV):
        off = SUBLANE - (SSD_CONV - 1) + k
        acc = acc + cw[k:k + 1, :] * full[off:off + q, :]
    xc = acc * _sigmoid(acc)
    xs = xc[:, :WIDTH]
    bm = xc[:, WIDTH:WIDTH + LANE]
    cm = xc[:, WIDTH + LANE:]

    dt = _softplus(dt_ref[...] + dtb_ref[...])
    adt = dt * (-jnp.exp(alog_ref[...]))
    a_cs = _cumsum_rows(adt)
    a_tot = a_cs[q - 1:q, :]
    xdt = xs * dt
    xw = (xdt * jnp.exp(a_tot - a_cs)).astype(BF16)
    dec_out = jnp.exp(a_cs)
    dec_tot = jnp.exp(a_tot)
    a_cs_t = a_cs.T
    xdt_b = xdt.astype(BF16)

    lo = lax.broadcasted_iota(jnp.int32, (1, LANE), 1) < SSD_STATE
    tril = (lax.broadcasted_iota(jnp.int32, (q, q), 0)
            >= lax.broadcasted_iota(jnp.int32, (q, q), 1))
    cm_g = (jnp.where(lo, cm, 0.0).astype(BF16), jnp.where(lo, 0.0, cm).astype(BF16))
    bm_g = (jnp.where(lo, bm, 0.0).astype(BF16), jnp.where(lo, 0.0, bm).astype(BF16))
    bm_b = bm.astype(BF16)
    nt = (((1,), (1,)), ((), ()))
    tn = (((0,), (0,)), ((), ()))
    cb = tuple(lax.dot_general(cm_g[g], bm_b, nt, preferred_element_type=F32) for g in (0, 1))

    ys = []
    for j in range(N_HEADS // 2):
        g = j // (N_HEADS // 2 // SSD_GROUPS)
        c0, c1 = j * LANE, (j + 1) * LANE
        yd = []
        for e in (0, 1):
            hl = (2 * j + e) * HEAD_DIM
            diff = a_cs[:, hl:hl + 1] - a_cs_t[hl:hl + 1, :]
            lm = jnp.exp(jnp.where(tril, diff, NEG_BIG))
            yd.append(jnp.dot((cb[g] * lm).astype(BF16), xdt_b[:, c0:c1],
                              preferred_element_type=F32))
        y_diag = jnp.where(lo, yd[0], yd[1])
        st = state_ref[j]
        y_off = jnp.dot(cm_g[g], st.astype(BF16), preferred_element_type=F32) * dec_out[:, c0:c1]
        state_ref[j] = dec_tot[:, c0:c1] * st + lax.dot_general(
            bm_g[g], xw[:, c0:c1], tn, preferred_element_type=F32)
        ys.append(y_diag + y_off)
    y = jnp.concatenate(ys, axis=1) + d_ref[...] * xs
    zf = z_ref[...]
    y = y * (zf * _sigmoid(zf))
    gw = WIDTH // SSD_GROUPS
    outs = []
    for g in range(SSD_GROUPS):
        yg = y[:, g * gw:(g + 1) * gw]
        outs.append(yg * lax.rsqrt(jnp.mean(yg * yg, axis=-1, keepdims=True) + RMS_EPS))
    o_ref[...] = (jnp.concatenate(outs, axis=1) * nw_ref[...]).astype(o_ref.dtype)


def _ssd(xbc, z, dt_rep, cw, cb, dtb_rep, alog_rep, d_rep, nw, bsz, seq):
    m = xbc.shape[0]
    q = SSD_CHUNK
    nc = seq // q
    row = lambda b, c: (b * nc + c, 0)
    halo = lambda b, c: (jnp.maximum((b * nc + c) * (q // SUBLANE) - 1, 0), 0)
    return pl.pallas_call(
        _ssd_kernel,
        grid=(bsz, nc),
        in_specs=[
            pl.BlockSpec((q, SSD_CONV_DIM), row),
            pl.BlockSpec((SUBLANE, SSD_CONV_DIM), halo),
            pl.BlockSpec((q, WIDTH), row),
            pl.BlockSpec((q, WIDTH), row),
            _full(cw.shape), _full(cb.shape), _full(dtb_rep.shape), _full(alog_rep.shape),
            _full(d_rep.shape), _full(nw.shape),
        ],
        out_specs=pl.BlockSpec((q, WIDTH), row),
        out_shape=jax.ShapeDtypeStruct((m, WIDTH), BF16),
        scratch_shapes=[pltpu.VMEM((N_HEADS // 2, LANE, LANE), F32)],
        compiler_params=_params(("parallel", "arbitrary")),
        name="ssd",
    )(xbc, xbc, z, dt_rep, cw, cb, dtb_rep, alog_rep, d_rep, nw)


def _s5_kernel(u_ref, wb_ref, wc_ref, are_ref, aim_ref, d_ref, wg_ref, bg_ref, o_ref,
               bu_ref, h_ref, *, steps, bsz):
    n = S5_LANES

    @pl.when(pl.program_id(0) == 0)
    def _():
        h_ref[...] = jnp.zeros_like(h_ref)

    u = u_ref[...]
    bu_ref[...] = jnp.dot(u.astype(BF16), wb_ref[...], preferred_element_type=F32)
    ar = are_ref[...]
    ai = aim_ref[...]

    def step(t, carry):
        hr, hi = carry
        r0 = pl.multiple_of(t * bsz, bsz)
        br = bu_ref[pl.ds(r0, bsz), 0:n]
        bi = bu_ref[pl.ds(r0, bsz), n:2 * n]
        nr = ar * hr - ai * hi + br
        ni = ar * hi + ai * hr + bi
        bu_ref[pl.ds(r0, bsz), 0:n] = nr
        bu_ref[pl.ds(r0, bsz), n:2 * n] = ni
        return nr, ni

    hr, hi = lax.fori_loop(0, steps, step, (h_ref[:, 0:n], h_ref[:, n:2 * n]))
    h_ref[:, 0:n] = hr
    h_ref[:, n:2 * n] = hi
    y = jnp.dot(bu_ref[...].astype(BF16), wc_ref[...], preferred_element_type=F32)
    y = y + d_ref[...] * u
    gy = jax.nn.gelu(y)
    gl = jnp.dot(gy.astype(BF16), wg_ref[...], preferred_element_type=F32) + bg_ref[...]
    o_ref[...] = (gy * _sigmoid(gl)).astype(o_ref.dtype)


def _s5(u_tb, wb, wc, a_re, a_im, d, wg, bg, bsz, seq, steps=32):
    rows = steps * bsz
    kernel = functools.partial(_s5_kernel, steps=steps, bsz=bsz)
    return pl.pallas_call(
        kernel,
        grid=(seq // steps,),
        in_specs=[
            pl.BlockSpec((rows, WIDTH), lambda i: (i, 0)),
            _full(wb.shape), _full(wc.shape), _full(a_re.shape), _full(a_im.shape),
            _full(d.shape), _full(wg.shape), _full(bg.shape),
        ],
        out_specs=pl.BlockSpec((rows, WIDTH), lambda i: (i, 0)),
        out_shape=jax.ShapeDtypeStruct(u_tb.shape, BF16),
        scratch_shapes=[pltpu.VMEM((rows, 2 * S5_LANES), F32),
                        pltpu.VMEM((bsz, 2 * S5_LANES), F32)],
        compiler_params=_params(("arbitrary",)),
        name="s5",
    )(u_tb, wb, wc, a_re, a_im, d, wg, bg)


def _s5_weights(a_re, a_im, b_re, b_im, c_re, c_im, log_step, bsz):
    lam_re = jnp.minimum(a_re, -1e-4)
    lam_im = a_im
    step = jnp.exp(log_step)[:, None]
    mag = jnp.exp(lam_re * step)
    abar_re = mag * jnp.cos(lam_im * step)
    abar_im = mag * jnp.sin(lam_im * step)
    den = lam_re * lam_re + lam_im * lam_im
    num_re = abar_re - 1.0
    k_re = (num_re * lam_re + abar_im * lam_im) / den
    k_im = (abar_im * lam_re - num_re * lam_im) / den
    bb_re = k_re[..., None] * b_re - k_im[..., None] * b_im
    bb_im = k_re[..., None] * b_im + k_im[..., None] * b_re
    eye = jnp.eye(S5_GROUPS, dtype=F32)

    def in_blocks(bb):
        t = jnp.transpose(bb, (0, 2, 1))[:, :, None, :] * eye[:, None, :, None]
        return t.reshape(WIDTH, S5_LANES)

    def out_blocks(cc):
        t = jnp.transpose(cc, (0, 2, 1))[:, :, None, :] * eye[:, None, :, None]
        return t.reshape(S5_LANES, WIDTH)

    wb = jnp.concatenate([in_blocks(bb_re), in_blocks(bb_im)], axis=1).astype(BF16)
    wc = jnp.concatenate([out_blocks(c_re), -out_blocks(c_im)], axis=0).astype(BF16)
    rep = lambda a: jnp.broadcast_to(a.reshape(1, S5_LANES), (bsz, S5_LANES))
    return wb, wc, rep(abar_re), rep(abar_im)


def _merge_kernel(ya_ref, yb_ref, yc_ref, gate_ref, x_ref, wbr_ref, bg_ref, wo_ref, g_ref, b_ref,
                  o_ref, *, alpha):
    merged = None
    for r, y_ref in enumerate((ya_ref, yb_ref, yc_ref)):
        pr = jnp.dot(y_ref[...], wbr_ref[r], preferred_element_type=F32)
        gt = _sigmoid(gate_ref[:, r * D_MODEL:(r + 1) * D_MODEL] + bg_ref[r:r + 1, :])
        merged = gt * pr if merged is None else merged + gt * pr
    mix = jnp.dot(merged.astype(BF16), wo_ref[...], preferred_element_type=F32)
    o_ref[...] = _layer_norm(alpha * x_ref[...] + mix, g_ref[...], b_ref[...])


def _merge(ya, yb, yc_tb, gates, x2, wbr, bg, wo, g, b, bsz, seq, alpha, tm=512):
    m = x2.shape[0]
    nt = seq // tm
    row = lambda i: (i, 0)
    kernel = functools.partial(_merge_kernel, alpha=alpha)
    return pl.pallas_call(
        kernel,
        grid=(m // tm,),
        in_specs=[
            pl.BlockSpec((tm, WIDTH), row),
            pl.BlockSpec((tm, WIDTH), row),
            pl.BlockSpec((tm, WIDTH), lambda i: (i % nt, i // nt)),
            pl.BlockSpec((tm, N_BRANCH * D_MODEL), row),
            pl.BlockSpec((tm, D_MODEL), row),
            _full(wbr.shape), _full(bg.shape), _full(wo.shape), _full(g.shape), _full(b.shape),
        ],
        out_specs=pl.BlockSpec((tm, D_MODEL), row),
        out_shape=jax.ShapeDtypeStruct((m, D_MODEL), F32),
        compiler_params=_params(("parallel",)),
        name="merge_ln1",
    )(ya, yb, yc_tb, gates, x2, wbr, bg, wo, g, b)


def _ffn_kernel(x_ref, halo_ref, wu_ref, cw_ref, cb_ref, wd_ref, g_ref, b_ref, o_ref, *,
                alpha, nt, tf):
    tm = x_ref.shape[0]
    first = (pl.program_id(0) % nt) == 0
    x = x_ref[...]
    halo = jnp.where(first, 0.0, halo_ref[...])
    xf = jnp.concatenate([halo, x], axis=0).astype(BF16)

    def conv(h, c0):
        acc = cb_ref[:, c0:c0 + tf]
        for k in range(FFN_CONV):
            off = SUBLANE - (FFN_CONV - 1) + k
            acc = acc + cw_ref[k:k + 1, c0:c0 + tf] * h[off:off + tm, :]
        return acc

    out = None
    for j in range(D_FF // tf):
        cv, cg = j * tf, D_FF + j * tf
        val = conv(jnp.dot(xf, wu_ref[:, cv:cv + tf], preferred_element_type=F32), cv)
        gate = conv(jnp.dot(xf, wu_ref[:, cg:cg + tf], preferred_element_type=F32), cg)
        act = (gate * _sigmoid(gate) * val).astype(BF16)
        part = jnp.dot(act, wd_ref[cv:cv + tf, :], preferred_element_type=F32)
        out = part if out is None else out + part
    o_ref[...] = _layer_norm(alpha * x + out, g_ref[...], b_ref[...])


def _ffn(x2, wu, cw, cb, wd, g, b, seq, alpha, tm=512, tf=256):
    m = x2.shape[0]
    nt = seq // tm
    row = lambda i: (i, 0)
    kernel = functools.partial(_ffn_kernel, alpha=alpha, nt=nt, tf=tf)
    return pl.pallas_call(
        kernel,
        grid=(m // tm,),
        in_specs=[
            pl.BlockSpec((tm, D_MODEL), row),
            pl.BlockSpec((SUBLANE, D_MODEL), lambda i: (jnp.maximum(i * (tm // SUBLANE) - 1, 0), 0)),
            _full(wu.shape), _full(cw.shape), _full(cb.shape), _full(wd.shape),
            _full(g.shape), _full(b.shape),
        ],
        out_specs=pl.BlockSpec((tm, D_MODEL), row),
        out_shape=jax.ShapeDtypeStruct((m, D_MODEL), F32),
        compiler_params=_params(("parallel",)),
        name="ffn_ln2",
    )(x2, x2, wu, cw, cb, wd, g, b)


def _in_proj_weight(w_in):
    o = 0
    parts = {}
    for name, size in (("q", WIDTH), ("k", WIDTH), ("v", WIDTH), ("f", N_HEADS), ("z", WIDTH),
                       ("xbc", SSD_CONV_DIM), ("dt", N_HEADS), ("u", WIDTH),
                       ("gate", N_BRANCH * D_MODEL)):
        parts[name] = w_in[:, o:o + size]
        o += size
    dt_rep = jnp.repeat(parts["dt"], HEAD_DIM, axis=1)
    f_pad = jnp.pad(parts["f"], ((0, 0), (0, LANE - N_HEADS)))
    w = jnp.concatenate([parts["q"] * (HEAD_DIM ** -0.5), parts["k"], parts["v"], parts["xbc"],
                         parts["z"], dt_rep, parts["u"], parts["gate"], f_pad], axis=1)
    return w.astype(BF16)


def kernel(x, w_in, fox_f_bias, ssd_conv_w, ssd_conv_b, ssd_dt_bias, ssd_a_log, ssd_d, ssd_norm_w, s5_a_re, s5_a_im, s5_b_re, s5_b_im, s5_c_re, s5_c_im, s5_log_step, s5_d, s5_w_glu, s5_b_glu, w_branch, b_gate, w_out, ln1_g, ln1_b, ffn_w_up, ffn_conv_w, ffn_conv_b, ffn_w_down, ln2_g, ln2_b):
    bsz, seq, _ = x.shape
    depth = w_in.shape[0]
    alpha = (2 * depth) ** 0.25
    tq = min(256, seq)
    nq = seq // tq
    rep = lambda a: jnp.repeat(a, HEAD_DIM).reshape(1, WIDTH)
    row = lambda a: a.reshape(1, -1)
    x2 = x.reshape(bsz * seq, D_MODEL)
    for i in range(depth):
        w = _in_proj_weight(w_in[i])
        qkv, xbc, z, dt_rep, u_tb, gates, fl = _in_proj(x2, w, bsz, seq)
        f_bias = jnp.pad(fox_f_bias[i], (0, LANE - N_HEADS)).reshape(1, LANE)
        c = _fox_cum(fl, f_bias, bsz, seq)
        ck = c[:, :SUBLANE].reshape(bsz, nq, tq, SUBLANE).transpose(0, 1, 3, 2)
        ya = _fox_attn(qkv, c, ck, bsz, seq, tq)
        yb = _ssd(xbc, z, dt_rep, ssd_conv_w[i], row(ssd_conv_b[i]), rep(ssd_dt_bias[i]),
                  rep(ssd_a_log[i]), rep(ssd_d[i]), row(ssd_norm_w[i]), bsz, seq)
        wb, wc, a_re, a_im = _s5_weights(s5_a_re[i], s5_a_im[i], s5_b_re[i], s5_b_im[i],
                                         s5_c_re[i], s5_c_im[i], s5_log_step[i], bsz)
        yc = _s5(u_tb.reshape(seq * bsz, WIDTH), wb, wc, a_re, a_im, row(s5_d[i]),
                 s5_w_glu[i].astype(BF16), row(s5_b_glu[i]), bsz, seq)
        x2 = _merge(ya, yb, yc.reshape(seq, bsz * WIDTH), gates, x2, w_branch[i].astype(BF16),
                    b_gate[i], w_out[i].astype(BF16), row(ln1_g[i]), row(ln1_b[i]), bsz, seq,
                    alpha)
        x2 = _ffn(x2, ffn_w_up[i].astype(BF16), ffn_conv_w[i], row(ffn_conv_b[i]),
                  ffn_w_down[i].astype(BF16), row(ln2_g[i]), row(ln2_b[i]), seq, alpha)
    return x2.reshape(bsz, seq, D_MODEL)
```

```python
import functools
import math

import jax
import jax.numpy as jnp
from jax import lax
from jax.experimental import pallas as pl
from jax.experimental.pallas import tpu as pltpu

F32 = jnp.float32
BF16 = jnp.bfloat16

D_MODEL = 1024
N_HEADS = 8
HEAD_DIM = 64
WIDTH = N_HEADS * HEAD_DIM
SSD_STATE = 64
SSD_GROUPS = 2
SSD_CONV = 4
SSD_CHUNK = 128
SSD_CONV_DIM = WIDTH + 2 * SSD_GROUPS * SSD_STATE
S5_GROUP = 16
S5_GROUPS = WIDTH // S5_GROUP
S5_STATE = 64
S5_LANES = S5_GROUPS * S5_STATE
D_FF = 2816
FFN_CONV = 3
N_BRANCH = 3
LN_EPS = 1e-5
RMS_EPS = 1e-5
LANE = 128
SUBLANE = 8
NEG_BIG = -1e30
VMEM_LIMIT = 56 * 1024 * 1024

QK_WIDE = N_HEADS * LANE
AUG0 = HEAD_DIM
C3_ONE = 3 * N_HEADS
LOG2E = math.log2(math.e)

_C_Q = 0
_C_K = _C_Q + QK_WIDE
_C_XBC = _C_K + QK_WIDE
_C_Z = _C_XBC + SSD_CONV_DIM
_C_DT = _C_Z + WIDTH
_C_U = _C_DT + WIDTH
_C_GATE = _C_U + WIDTH
_C_FL = _C_GATE + N_BRANCH * D_MODEL
_C_END = _C_FL + LANE


def _params(sem):
    return pltpu.CompilerParams(dimension_semantics=sem, vmem_limit_bytes=VMEM_LIMIT)


def _full(shape):
    n = len(shape)
    return pl.BlockSpec(shape, lambda *_: (0,) * n)


def _sigmoid(x):
    return 1.0 / (1.0 + jnp.exp(-x))


def _softplus(x):
    return jnp.maximum(x, 0.0) + jnp.log1p(jnp.exp(-jnp.abs(x)))


def _cumsum_rows(x):
    n = x.shape[0]
    row = lax.broadcasted_iota(jnp.int32, x.shape, 0)
    s = 1
    while s < n:
        x = x + jnp.where(row >= s, pltpu.roll(x, s, 0), 0.0)
        s *= 2
    return x


def _layer_norm(v, g, b):
    mu = jnp.mean(v, axis=-1, keepdims=True)
    d = v - mu
    var = jnp.mean(d * d, axis=-1, keepdims=True)
    return d * lax.rsqrt(var + LN_EPS) * g + b


def _in_proj_kernel(x_ref, w_ref, wvt_ref, q_ref, k_ref, vt_ref, xbc_ref, z_ref, dt_ref, u_ref,
                    gate_ref, fl_ref):
    xb = x_ref[...].astype(BF16)

    def mm(c0, c1):
        return jnp.dot(xb, w_ref[:, c0:c1], preferred_element_type=F32)

    q_ref[...] = mm(_C_Q, _C_K).astype(BF16)
    k_ref[...] = mm(_C_K, _C_XBC).astype(BF16)
    vt_ref[...] = lax.dot_general(wvt_ref[...], xb, (((1,), (1,)), ((), ())),
                                  preferred_element_type=F32).astype(BF16)
    xbc_ref[...] = mm(_C_XBC, _C_Z)
    z_ref[...] = mm(_C_Z, _C_DT)
    dt_ref[...] = mm(_C_DT, _C_U)
    u_ref[...] = mm(_C_U, _C_GATE)
    gate_ref[...] = mm(_C_GATE, _C_FL)
    fl_ref[...] = mm(_C_FL, _C_END)


def _in_proj(x2, w, wvt, bsz, seq, tm=256):
    m = x2.shape[0]
    nt = seq // tm
    row = lambda i: (i, 0)
    out_shape = (
        jax.ShapeDtypeStruct((m, QK_WIDE), BF16),
        jax.ShapeDtypeStruct((m, QK_WIDE), BF16),
        jax.ShapeDtypeStruct((m // tm, WIDTH, tm), BF16),
        jax.ShapeDtypeStruct((m, SSD_CONV_DIM), F32),
        jax.ShapeDtypeStruct((m, WIDTH), F32),
        jax.ShapeDtypeStruct((m, WIDTH), F32),
        jax.ShapeDtypeStruct((seq, bsz * WIDTH), F32),
        jax.ShapeDtypeStruct((m, N_BRANCH * D_MODEL), F32),
        jax.ShapeDtypeStruct((m, LANE), F32),
    )
    out_specs = (
        pl.BlockSpec((tm, QK_WIDE), row),
        pl.BlockSpec((tm, QK_WIDE), row),
        pl.BlockSpec((None, WIDTH, tm), lambda i: (i, 0, 0)),
        pl.BlockSpec((tm, SSD_CONV_DIM), row),
        pl.BlockSpec((tm, WIDTH), row),
        pl.BlockSpec((tm, WIDTH), row),
        pl.BlockSpec((tm, WIDTH), lambda i: (i % nt, i // nt)),
        pl.BlockSpec((tm, N_BRANCH * D_MODEL), row),
        pl.BlockSpec((tm, LANE), row),
    )
    return pl.pallas_call(
        _in_proj_kernel,
        grid=(m // tm,),
        in_specs=[pl.BlockSpec((tm, D_MODEL), row), _full(w.shape), _full(wvt.shape)],
        out_specs=out_specs,
        out_shape=out_shape,
        compiler_params=_params(("parallel",)),
        name="in_proj",
    )(x2, w, wvt)


def _fox_cum_kernel(fl_ref, bias_ref, c3_ref):
    v = fl_ref[...] + bias_ref[...]
    ls = jnp.minimum(v, 0.0) - jnp.log1p(jnp.exp(-jnp.abs(v)))
    c = _cumsum_rows(ls) * LOG2E
    hi = c.astype(BF16).astype(F32)
    r1 = c - hi
    mid = r1.astype(BF16).astype(F32)
    lo = (r1 - mid).astype(BF16).astype(F32)
    lane = lax.broadcasted_iota(jnp.int32, c.shape, 1)
    out = jnp.where(lane < N_HEADS, hi,
                    jnp.where(lane < 2 * N_HEADS, pltpu.roll(mid, N_HEADS, 1),
                              jnp.where(lane < C3_ONE, pltpu.roll(lo, 2 * N_HEADS, 1),
                                        jnp.where(lane == C3_ONE, 1.0, 0.0))))
    c3_ref[...] = out.astype(BF16)


def _fox_cum(fl, bias_pad, bsz, seq):
    return pl.pallas_call(
        _fox_cum_kernel,
        grid=(bsz,),
        in_specs=[pl.BlockSpec((seq, LANE), lambda b: (b, 0)), _full((1, LANE))],
        out_specs=pl.BlockSpec((seq, LANE), lambda b: (b, 0)),
        out_shape=jax.ShapeDtypeStruct(fl.shape, BF16),
        compiler_params=_params(("parallel",)),
        name="fox_cum",
    )(fl, bias_pad)


def _bias_placement():
    import numpy as np
    pq = np.zeros((LANE, QK_WIDE), np.float32)
    pk = np.zeros((LANE, QK_WIDE), np.float32)
    for h in range(N_HEADS):
        base = h * LANE + AUG0
        for piece in range(3):
            pq[piece * N_HEADS + h, base + piece] = 1.0
            pq[C3_ONE, base + 3 + piece] = 1.0
            pk[C3_ONE, base + piece] = 1.0
            pk[piece * N_HEADS + h, base + 3 + piece] = -1.0
    return jnp.asarray(pq, BF16), jnp.asarray(pk, BF16)


def _fox_attn_kernel(q_ref, k_ref, vt_ref, c3q_ref, c3k_ref, pq_ref, pk_ref, o_ref, ka_ref, s_ref, *,
                     tq, rows_k):
    qi = pl.program_id(1)
    seq = k_ref.shape[0]

    @pl.when(qi == 0)
    def _():
        def fill(i, _):
            r0 = pl.multiple_of(i * rows_k, rows_k)
            aug = jnp.dot(c3k_ref[pl.ds(r0, rows_k), :], pk_ref[...], preferred_element_type=F32)
            ka_ref[pl.ds(r0, rows_k), :] = (k_ref[pl.ds(r0, rows_k), :].astype(F32)
                                            + aug).astype(BF16)
            return 0
        lax.fori_loop(0, seq // rows_k, fill, 0)

    qa = (q_ref[...].astype(F32)
          + jnp.dot(c3q_ref[...], pq_ref[...], preferred_element_type=F32)).astype(BF16)
    causal = (lax.broadcasted_iota(jnp.int32, (tq, tq), 0)
              <= lax.broadcasted_iota(jnp.int32, (tq, tq), 1))
    nt = (((1,), (1,)), ((), ()))

    def tree(fn, s):
        parts = [s[i * SUBLANE:(i + 1) * SUBLANE] for i in range(s.shape[0] // SUBLANE)]
        while len(parts) > 1:
            parts = [fn(parts[i], parts[i + 1]) for i in range(0, len(parts), 2)]
        return parts[0]

    hpb = s_ref.shape[0]
    for j in range(N_HEADS // hpb):
        heads = tuple(range(j * hpb, (j + 1) * hpb))
        qh = tuple(qa[:, h * LANE:(h + 1) * LANE] for h in heads)

        def scores(ki, heads=heads, qh=qh):
            for e in range(hpb):
                score1(ki, e)

        def score1(ki, e, heads=heads, qh=qh):
            r0 = pl.multiple_of(ki * tq, tq)
            h = heads[e]
            s_ref[e] = lax.dot_general(ka_ref[pl.ds(r0, tq), h * LANE:(h + 1) * LANE], qh[e], nt,
                                       preferred_element_type=F32)

        def absorb(ki, stats, masked, nxt, heads=heads):
            pend = []
            for e, h in enumerate(heads):
                m, l, acc = stats[e]
                se = jnp.where(causal, s_ref[e], NEG_BIG) if masked else s_ref[e]
                m_new = jnp.maximum(m, jnp.max(tree(jnp.maximum, se), axis=0, keepdims=True))
                p = jnp.exp2(se - m_new)
                alpha = jnp.exp2(m - m_new)
                l = alpha * l + jnp.sum(tree(jnp.add, p), axis=0, keepdims=True)
                vth = vt_ref[ki, h * HEAD_DIM:(h + 1) * HEAD_DIM, :]
                pv = jnp.dot(vth, p.astype(BF16), preferred_element_type=F32)
                if nxt is not None:
                    score1(nxt, e)
                pend.append((m_new, l, alpha, acc, pv))
            return tuple((m_new, l, alpha * acc + pv) for m_new, l, alpha, acc, pv in pend)

        init1 = (jnp.full((1, tq), NEG_BIG, F32), jnp.zeros((1, tq), F32),
                 jnp.zeros((HEAD_DIM, tq), F32))
        scores(0)
        stats = lax.fori_loop(0, qi, lambda ki, st: absorb(ki, st, False, ki + 1),
                              (init1,) * hpb)
        stats = absorb(qi, stats, True, None)
        for e in range(0, hpb, 2):
            (_, l0, a0), (_, l1, a1) = stats[e], stats[e + 1]
            out_t = jnp.concatenate([a0 / l0, a1 / l1], axis=0)
            c0 = (heads[e] // 2) * LANE
            o_ref[:, c0:c0 + LANE] = out_t.T.astype(o_ref.dtype)


def _fox_attn(q, k, vt, c3, pq, pk, bsz, seq, tq, hpb=4):
    m = q.shape[0]
    nq = seq // tq
    vt4 = vt.reshape(bsz, nq, WIDTH, tq)
    kernel = functools.partial(_fox_attn_kernel, tq=tq, rows_k=min(512, seq))
    return pl.pallas_call(
        kernel,
        grid=(bsz, nq),
        in_specs=[
            pl.BlockSpec((tq, QK_WIDE), lambda b, i: (b * nq + i, 0)),
            pl.BlockSpec((seq, QK_WIDE), lambda b, i: (b, 0)),
            pl.BlockSpec((None, nq, WIDTH, tq), lambda b, i: (b, 0, 0, 0)),
            pl.BlockSpec((tq, LANE), lambda b, i: (b * nq + i, 0)),
            pl.BlockSpec((seq, LANE), lambda b, i: (b, 0)),
            _full(pq.shape), _full(pk.shape),
        ],
        out_specs=pl.BlockSpec((tq, WIDTH), lambda b, i: (b * nq + i, 0)),
        out_shape=jax.ShapeDtypeStruct((m, WIDTH), BF16),
        scratch_shapes=[pltpu.VMEM((seq, QK_WIDE), BF16), pltpu.VMEM((hpb, tq, tq), F32)],
        compiler_params=_params(("parallel", "arbitrary")),
        name="fox_attn",
    )(q, k, vt4, c3, c3, pq, pk)


def _ssd_kernel(xbc_ref, halo_ref, z_ref, dt_ref, cw_ref, cb_ref, dtb_ref, alog_ref, d_ref,
                nw_ref, o_ref, state_ref):
    c = pl.program_id(1)
    q = SSD_CHUNK

    @pl.when(c == 0)
    def _():
        state_ref[...] = jnp.zeros_like(state_ref)

    halo = jnp.where(c > 0, halo_ref[...], 0.0)
    full = jnp.concatenate([halo, xbc_ref[...]], axis=0)
    cw = cw_ref[...]
    acc = cb_ref[...]
    for k in range(SSD_CONV):
        off = SUBLANE - (SSD_CONV - 1) + k
        acc = acc + cw[k:k + 1, :] * full[off:off + q, :]
    xc = acc * _sigmoid(acc)
    xs = xc[:, :WIDTH]
    bm = xc[:, WIDTH:WIDTH + LANE]
    cm = xc[:, WIDTH + LANE:]

    dt = _softplus(dt_ref[...] + dtb_ref[...])
    adt = dt * (-jnp.exp(alog_ref[...]))
    a_cs = _cumsum_rows(adt)
    a_tot = a_cs[q - 1:q, :]
    xdt = xs * dt
    xw = (xdt * jnp.exp(a_tot - a_cs)).astype(BF16)
    dec_out = jnp.exp(a_cs)
    dec_tot = jnp.exp(a_tot)
    a_cs_t = a_cs.T
    xdt_b = xdt.astype(BF16)

    lo = lax.broadcasted_iota(jnp.int32, (1, LANE), 1) < SSD_STATE
    tril = (lax.broadcasted_iota(jnp.int32, (q, q), 0)
            >= lax.broadcasted_iota(jnp.int32, (q, q), 1))
    cm_g = (jnp.where(lo, cm, 0.0).astype(BF16), jnp.where(lo, 0.0, cm).astype(BF16))
    bm_g = (jnp.where(lo, bm, 0.0).astype(BF16), jnp.where(lo, 0.0, bm).astype(BF16))
    bm_b = bm.astype(BF16)
    nt = (((1,), (1,)), ((), ()))
    tn = (((0,), (0,)), ((), ()))
    cb = tuple(lax.dot_general(cm_g[g], bm_b, nt, preferred_element_type=F32) for g in (0, 1))

    ys = []
    for j in range(N_HEADS // 2):
        g = j // (N_HEADS // 2 // SSD_GROUPS)
        c0, c1 = j * LANE, (j + 1) * LANE
        yd = []
        for e in (0, 1):
            hl = (2 * j + e) * HEAD_DIM
            diff = a_cs[:, hl:hl + 1] - a_cs_t[hl:hl + 1, :]
            lm = jnp.exp(jnp.where(tril, diff, NEG_BIG))
            yd.append(jnp.dot((cb[g] * lm).astype(BF16), xdt_b[:, c0:c1],
                              preferred_element_type=F32))
        y_diag = jnp.where(lo, yd[0], yd[1])
        st = state_ref[j]
        y_off = jnp.dot(cm_g[g], st.astype(BF16), preferred_element_type=F32) * dec_out[:, c0:c1]
        state_ref[j] = dec_tot[:, c0:c1] * st + lax.dot_general(
            bm_g[g], xw[:, c0:c1], tn, preferred_element_type=F32)
        ys.append(y_diag + y_off)
    y = jnp.concatenate(ys, axis=1) + d_ref[...] * xs
    zf = z_ref[...]
    y = y * (zf * _sigmoid(zf))
    gw = WIDTH // SSD_GROUPS
    outs = []
    for g in range(SSD_GROUPS):
        yg = y[:, g * gw:(g + 1) * gw]
        outs.append(yg * lax.rsqrt(jnp.mean(yg * yg, axis=-1, keepdims=True) + RMS_EPS))
    o_ref[...] = (jnp.concatenate(outs, axis=1) * nw_ref[...]).astype(o_ref.dtype)


def _ssd(xbc, z, dt_rep, cw, cb, dtb_rep, alog_rep, d_rep, nw, bsz, seq):
    m = xbc.shape[0]
    q = SSD_CHUNK
    nc = seq // q
    row = lambda b, c: (b * nc + c, 0)
    halo = lambda b, c: (jnp.maximum((b * nc + c) * (q // SUBLANE) - 1, 0), 0)
    return pl.pallas_call(
        _ssd_kernel,
        grid=(bsz, nc),
        in_specs=[
            pl.BlockSpec((q, SSD_CONV_DIM), row),
            pl.BlockSpec((SUBLANE, SSD_CONV_DIM), halo),
            pl.BlockSpec((q, WIDTH), row),
            pl.BlockSpec((q, WIDTH), row),
            _full(cw.shape), _full(cb.shape), _full(dtb_rep.shape), _full(alog_rep.shape),
            _full(d_rep.shape), _full(nw.shape),
        ],
        out_specs=pl.BlockSpec((q, WIDTH), row),
        out_shape=jax.ShapeDtypeStruct((m, WIDTH), BF16),
        scratch_shapes=[pltpu.VMEM((N_HEADS // 2, LANE, LANE), F32)],
        compiler_params=_params(("parallel", "arbitrary")),
        name="ssd",
    )(xbc, xbc, z, dt_rep, cw, cb, dtb_rep, alog_rep, d_rep, nw)


def _s5_kernel(u_ref, wb_ref, wc_ref, are_ref, aim_ref, d_ref, wg_ref, bg_ref, o_ref,
               bu_ref, h_ref, *, steps, bsz):
    n = S5_LANES

    @pl.when(pl.program_id(0) == 0)
    def _():
        h_ref[...] = jnp.zeros_like(h_ref)

    u = u_ref[...]
    bu_ref[...] = jnp.dot(u.astype(BF16), wb_ref[...], preferred_element_type=F32)
    ar = are_ref[...]
    ai = aim_ref[...]

    def step(t, carry):
        hr, hi = carry
        r0 = pl.multiple_of(t * bsz, bsz)
        br = bu_ref[pl.ds(r0, bsz), 0:n]
        bi = bu_ref[pl.ds(r0, bsz), n:2 * n]
        nr = ar * hr - ai * hi + br
        ni = ar * hi + ai * hr + bi
        bu_ref[pl.ds(r0, bsz), 0:n] = nr
        bu_ref[pl.ds(r0, bsz), n:2 * n] = ni
        return nr, ni

    hr, hi = lax.fori_loop(0, steps, step, (h_ref[:, 0:n], h_ref[:, n:2 * n]))
    h_ref[:, 0:n] = hr
    h_ref[:, n:2 * n] = hi
    y = jnp.dot(bu_ref[...].astype(BF16), wc_ref[...], preferred_element_type=F32)
    y = y + d_ref[...] * u
    gy = jax.nn.gelu(y)
    gl = jnp.dot(gy.astype(BF16), wg_ref[...], preferred_element_type=F32) + bg_ref[...]
    o_ref[...] = (gy * _sigmoid(gl)).astype(o_ref.dtype)


def _s5(u_tb, wb, wc, a_re, a_im, d, wg, bg, bsz, seq, steps=32):
    rows = steps * bsz
    kernel = functools.partial(_s5_kernel, steps=steps, bsz=bsz)
    return pl.pallas_call(
        kernel,
        grid=(seq // steps,),
        in_specs=[
            pl.BlockSpec((rows, WIDTH), lambda i: (i, 0)),
            _full(wb.shape), _full(wc.shape), _full(a_re.shape), _full(a_im.shape),
            _full(d.shape), _full(wg.shape), _full(bg.shape),
        ],
        out_specs=pl.BlockSpec((rows, WIDTH), lambda i: (i, 0)),
        out_shape=jax.ShapeDtypeStruct(u_tb.shape, BF16),
        scratch_shapes=[pltpu.VMEM((rows, 2 * S5_LANES), F32),
                        pltpu.VMEM((bsz, 2 * S5_LANES), F32)],
        compiler_params=_params(("arbitrary",)),
        name="s5",
    )(u_tb, wb, wc, a_re, a_im, d, wg, bg)


def _s5_weights(a_re, a_im, b_re, b_im, c_re, c_im, log_step, bsz):
    lam_re = jnp.minimum(a_re, -1e-4)
    lam_im = a_im
    step = jnp.exp(log_step)[:, None]
    mag = jnp.exp(lam_re * step)
    abar_re = mag * jnp.cos(lam_im * step)
    abar_im = mag * jnp.sin(lam_im * step)
    den = lam_re * lam_re + lam_im * lam_im
    num_re = abar_re - 1.0
    k_re = (num_re * lam_re + abar_im * lam_im) / den
    k_im = (abar_im * lam_re - num_re * lam_im) / den
    bb_re = k_re[..., None] * b_re - k_im[..., None] * b_im
    bb_im = k_re[..., None] * b_im + k_im[..., None] * b_re
    eye = jnp.eye(S5_GROUPS, dtype=F32)

    def in_blocks(bb):
        t = jnp.transpose(bb, (0, 2, 1))[:, :, None, :] * eye[:, None, :, None]
        return t.reshape(WIDTH, S5_LANES)

    def out_blocks(cc):
        t = jnp.transpose(cc, (0, 2, 1))[:, :, None, :] * eye[:, None, :, None]
        return t.reshape(S5_LANES, WIDTH)

    wb = jnp.concatenate([in_blocks(bb_re), in_blocks(bb_im)], axis=1).astype(BF16)
    wc = jnp.concatenate([out_blocks(c_re), -out_blocks(c_im)], axis=0).astype(BF16)
    rep = lambda a: jnp.broadcast_to(a.reshape(1, S5_LANES), (bsz, S5_LANES))
    return wb, wc, rep(abar_re), rep(abar_im)


def _merge_kernel(ya_ref, yb_ref, yc_ref, gate_ref, x_ref, wbr_ref, bg_ref, wo_ref, g_ref, b_ref,
                  o_ref, *, alpha):
    merged = None
    for r, y_ref in enumerate((ya_ref, yb_ref, yc_ref)):
        pr = jnp.dot(y_ref[...], wbr_ref[r], preferred_element_type=F32)
        gt = _sigmoid(gate_ref[:, r * D_MODEL:(r + 1) * D_MODEL] + bg_ref[r:r + 1, :])
        merged = gt * pr if merged is None else merged + gt * pr
    mix = jnp.dot(merged.astype(BF16), wo_ref[...], preferred_element_type=F32)
    o_ref[...] = _layer_norm(alpha * x_ref[...] + mix, g_ref[...], b_ref[...])


def _merge(ya, yb, yc_tb, gates, x2, wbr, bg, wo, g, b, bsz, seq, alpha, tm=512):
    m = x2.shape[0]
    nt = seq // tm
    row = lambda i: (i, 0)
    kernel = functools.partial(_merge_kernel, alpha=alpha)
    return pl.pallas_call(
        kernel,
        grid=(m // tm,),
        in_specs=[
            pl.BlockSpec((tm, WIDTH), row),
            pl.BlockSpec((tm, WIDTH), row),
            pl.BlockSpec((tm, WIDTH), lambda i: (i % nt, i // nt)),
            pl.BlockSpec((tm, N_BRANCH * D_MODEL), row),
            pl.BlockSpec((tm, D_MODEL), row),
            _full(wbr.shape), _full(bg.shape), _full(wo.shape), _full(g.shape), _full(b.shape),
        ],
        out_specs=pl.BlockSpec((tm, D_MODEL), row),
        out_shape=jax.ShapeDtypeStruct((m, D_MODEL), F32),
        compiler_params=_params(("parallel",)),
        name="merge_ln1",
    )(ya, yb, yc_tb, gates, x2, wbr, bg, wo, g, b)


def _ffn_kernel(x_ref, halo_ref, wu_ref, cw_ref, cb_ref, wd_ref, g_ref, b_ref, o_ref, *,
                alpha, nt, tf):
    tm = x_ref.shape[0]
    first = (pl.program_id(0) % nt) == 0
    x = x_ref[...]
    halo = jnp.where(first, 0.0, halo_ref[...])
    xf = jnp.concatenate([halo, x], axis=0).astype(BF16)

    def conv(h, c0):
        acc = cb_ref[:, c0:c0 + tf]
        for k in range(FFN_CONV):
            off = SUBLANE - (FFN_CONV - 1) + k
            acc = acc + cw_ref[k:k + 1, c0:c0 + tf] * h[off:off + tm, :]
        return acc

    out = None
    for j in range(D_FF // tf):
        cv, cg = j * tf, D_FF + j * tf
        val = conv(jnp.dot(xf, wu_ref[:, cv:cv + tf], preferred_element_type=F32), cv)
        gate = conv(jnp.dot(xf, wu_ref[:, cg:cg + tf], preferred_element_type=F32), cg)
        act = (gate * _sigmoid(gate) * val).astype(BF16)
        part = jnp.dot(act, wd_ref[cv:cv + tf, :], preferred_element_type=F32)
        out = part if out is None else out + part
    o_ref[...] = _layer_norm(alpha * x + out, g_ref[...], b_ref[...])


def _ffn(x2, wu, cw, cb, wd, g, b, seq, alpha, tm=512, tf=256):
    m = x2.shape[0]
    nt = seq // tm
    row = lambda i: (i, 0)
    kernel = functools.partial(_ffn_kernel, alpha=alpha, nt=nt, tf=tf)
    return pl.pallas_call(
        kernel,
        grid=(m // tm,),
        in_specs=[
            pl.BlockSpec((tm, D_MODEL), row),
            pl.BlockSpec((SUBLANE, D_MODEL), lambda i: (jnp.maximum(i * (tm // SUBLANE) - 1, 0), 0)),
            _full(wu.shape), _full(cw.shape), _full(cb.shape), _full(wd.shape),
            _full(g.shape), _full(b.shape),
        ],
        out_specs=pl.BlockSpec((tm, D_MODEL), row),
        out_shape=jax.ShapeDtypeStruct((m, D_MODEL), F32),
        compiler_params=_params(("parallel",)),
        name="ffn_ln2",
    )(x2, x2, wu, cw, cb, wd, g, b)


def _in_proj_weight(w_in):
    o = 0
    parts = {}
    for name, size in (("q", WIDTH), ("k", WIDTH), ("v", WIDTH), ("f", N_HEADS), ("z", WIDTH),
                       ("xbc", SSD_CONV_DIM), ("dt", N_HEADS), ("u", WIDTH),
                       ("gate", N_BRANCH * D_MODEL)):
        parts[name] = w_in[:, o:o + size]
        o += size

    def wide(w):
        w = w.reshape(D_MODEL, N_HEADS, HEAD_DIM)
        return jnp.pad(w, ((0, 0), (0, 0), (0, LANE - HEAD_DIM))).reshape(D_MODEL, QK_WIDE)

    dt_rep = jnp.repeat(parts["dt"], HEAD_DIM, axis=1)
    f_pad = jnp.pad(parts["f"], ((0, 0), (0, LANE - N_HEADS)))
    q_scale = HEAD_DIM ** -0.5 * LOG2E
    w = jnp.concatenate([wide(parts["q"] * q_scale), wide(parts["k"]), parts["xbc"], parts["z"],
                         dt_rep, parts["u"], parts["gate"], f_pad], axis=1)
    return w.astype(BF16), parts["v"].T.astype(BF16)


def kernel(x, w_in, fox_f_bias, ssd_conv_w, ssd_conv_b, ssd_dt_bias, ssd_a_log, ssd_d, ssd_norm_w, s5_a_re, s5_a_im, s5_b_re, s5_b_im, s5_c_re, s5_c_im, s5_log_step, s5_d, s5_w_glu, s5_b_glu, w_branch, b_gate, w_out, ln1_g, ln1_b, ffn_w_up, ffn_conv_w, ffn_conv_b, ffn_w_down, ln2_g, ln2_b):
    bsz, seq, _ = x.shape
    depth = w_in.shape[0]
    alpha = (2 * depth) ** 0.25
    tq = min(256, seq)
    pq, pk = _bias_placement()
    rep = lambda a: jnp.repeat(a, HEAD_DIM).reshape(1, WIDTH)
    row = lambda a: a.reshape(1, -1)
    x2 = x.reshape(bsz * seq, D_MODEL)
    for i in range(depth):
        w, wvt = _in_proj_weight(w_in[i])
        q, k, vt, xbc, z, dt_rep, u_tb, gates, fl = _in_proj(x2, w, wvt, bsz, seq, tq)
        f_bias = jnp.pad(fox_f_bias[i], (0, LANE - N_HEADS)).reshape(1, LANE)
        c3 = _fox_cum(fl, f_bias, bsz, seq)
        ya = _fox_attn(q, k, vt, c3, pq, pk, bsz, seq, tq)
        yb = _ssd(xbc, z, dt_rep, ssd_conv_w[i], row(ssd_conv_b[i]), rep(ssd_dt_bias[i]),
                  rep(ssd_a_log[i]), rep(ssd_d[i]), row(ssd_norm_w[i]), bsz, seq)
        wb, wc, a_re, a_im = _s5_weights(s5_a_re[i], s5_a_im[i], s5_b_re[i], s5_b_im[i],
                                         s5_c_re[i], s5_c_im[i], s5_log_step[i], bsz)
        yc = _s5(u_tb.reshape(seq * bsz, WIDTH), wb, wc, a_re, a_im, row(s5_d[i]),
                 s5_w_glu[i].astype(BF16), row(s5_b_glu[i]), bsz, seq)
        x2 = _merge(ya, yb, yc.reshape(seq, bsz * WIDTH), gates, x2, w_branch[i].astype(BF16),
                    b_gate[i], w_out[i].astype(BF16), row(ln1_g[i]), row(ln1_b[i]), bsz, seq,
                    alpha)
        x2 = _ffn(x2, ffn_w_up[i].astype(BF16), ffn_conv_w[i], row(ffn_conv_b[i]),
                  ffn_w_down[i].astype(BF16), row(ln2_g[i]), row(ln2_b[i]), seq, alpha)
    return x2.reshape(bsz, seq, D_MODEL)
```

```python
import functools
import math

import jax
import jax.numpy as jnp
from jax import lax
from jax.experimental import pallas as pl
from jax.experimental.pallas import tpu as pltpu

F32 = jnp.float32
BF16 = jnp.bfloat16

D_MODEL = 1024
N_HEADS = 8
HEAD_DIM = 64
WIDTH = N_HEADS * HEAD_DIM
SSD_STATE = 64
SSD_GROUPS = 2
SSD_CONV = 4
SSD_CHUNK = 128
SSD_CONV_DIM = WIDTH + 2 * SSD_GROUPS * SSD_STATE
S5_GROUP = 16
S5_GROUPS = WIDTH // S5_GROUP
S5_STATE = 64
S5_LANES = S5_GROUPS * S5_STATE
S5_SLAB_GROUPS = 16
D_FF = 2816
FFN_CONV = 3
N_BRANCH = 3
LN_EPS = 1e-5
RMS_EPS = 1e-5
LANE = 128
SUBLANE = 8
NEG_BIG = -1e30
VMEM_LIMIT = 56 * 1024 * 1024

QK_WIDE = N_HEADS * LANE
AUG0 = HEAD_DIM
C3_ONE = 3 * N_HEADS
LOG2E = math.log2(math.e)
ONES_ROWS = 16

_C_Q = 0
_C_K = _C_Q + QK_WIDE
_C_XBC = _C_K + QK_WIDE
_C_Z = _C_XBC + SSD_CONV_DIM
_C_DT = _C_Z + WIDTH
_C_U = _C_DT + WIDTH
_C_GATE = _C_U + WIDTH
_C_FL = _C_GATE + N_BRANCH * D_MODEL
_C_END = _C_FL + LANE


def _params(sem):
    return pltpu.CompilerParams(dimension_semantics=sem, vmem_limit_bytes=VMEM_LIMIT)


def _full(shape):
    n = len(shape)
    return pl.BlockSpec(shape, lambda *_: (0,) * n, pipeline_mode=pl.Buffered(1))


def _sigmoid(x):
    return 1.0 / (1.0 + jnp.exp(-x))


def _softplus(x):
    return jnp.maximum(x, 0.0) + jnp.log1p(jnp.exp(-jnp.abs(x)))


def _cumsum_rows(x):
    n = x.shape[0]
    row = lax.broadcasted_iota(jnp.int32, x.shape, 0)
    s = 1
    while s < n:
        x = x + jnp.where(row >= s, pltpu.roll(x, s, 0), 0.0)
        s *= 2
    return x


def _layer_norm(v, g, b):
    mu = jnp.mean(v, axis=-1, keepdims=True)
    d = v - mu
    var = jnp.mean(d * d, axis=-1, keepdims=True)
    return d * lax.rsqrt(var + LN_EPS) * g + b


def _in_proj_kernel(x_ref, w_ref, wvt_ref, q_ref, k_ref, vt_ref, xbc_ref, z_ref, dt_ref, u_ref,
                    gate_ref, fl_ref):
    xb = x_ref[...].astype(BF16)
    tk = vt_ref.shape[2]

    def mm(c0, c1):
        return jnp.dot(xb, w_ref[:, c0:c1], preferred_element_type=F32)

    q_ref[...] = mm(_C_Q, _C_K).astype(BF16)
    k_ref[...] = mm(_C_K, _C_XBC).astype(BF16)
    for r in range(vt_ref.shape[0]):
        vt_ref[r] = lax.dot_general(wvt_ref[...], xb[r * tk:(r + 1) * tk], (((1,), (1,)), ((), ())),
                                    preferred_element_type=F32).astype(BF16)
    xbc_ref[...] = mm(_C_XBC, _C_Z)
    z_ref[...] = mm(_C_Z, _C_DT)
    dt_ref[...] = mm(_C_DT, _C_U)
    u_ref[...] = mm(_C_U, _C_GATE)
    gate_ref[...] = mm(_C_GATE, _C_FL).astype(BF16)
    fl_ref[...] = mm(_C_FL, _C_END)


def _in_proj(x2, w, wvt, tk, tm=512):
    m = x2.shape[0]
    row = lambda i: (i, 0)
    out_shape = (
        jax.ShapeDtypeStruct((m, QK_WIDE), BF16),
        jax.ShapeDtypeStruct((m, QK_WIDE), BF16),
        jax.ShapeDtypeStruct((m // tk, WIDTH, tk), BF16),
        jax.ShapeDtypeStruct((m, SSD_CONV_DIM), F32),
        jax.ShapeDtypeStruct((m, WIDTH), F32),
        jax.ShapeDtypeStruct((m, WIDTH), F32),
        jax.ShapeDtypeStruct((m, WIDTH), F32),
        jax.ShapeDtypeStruct((m, N_BRANCH * D_MODEL), BF16),
        jax.ShapeDtypeStruct((m, LANE), F32),
    )
    out_specs = (
        pl.BlockSpec((tm, QK_WIDE), row),
        pl.BlockSpec((tm, QK_WIDE), row),
        pl.BlockSpec((tm // tk, WIDTH, tk), lambda i: (i, 0, 0)),
        pl.BlockSpec((tm, SSD_CONV_DIM), row),
        pl.BlockSpec((tm, WIDTH), row),
        pl.BlockSpec((tm, WIDTH), row),
        pl.BlockSpec((tm, WIDTH), row),
        pl.BlockSpec((tm, N_BRANCH * D_MODEL), row),
        pl.BlockSpec((tm, LANE), row),
    )
    return pl.pallas_call(
        _in_proj_kernel,
        grid=(m // tm,),
        in_specs=[pl.BlockSpec((tm, D_MODEL), row), _full(w.shape), _full(wvt.shape)],
        out_specs=out_specs,
        out_shape=out_shape,
        compiler_params=_params(("parallel",)),
        name="in_proj",
    )(x2, w, wvt)


def _fox_cum_kernel(fl_ref, bias_ref, c3_ref):
    v = fl_ref[...] + bias_ref[...]
    ls = jnp.minimum(v, 0.0) - jnp.log1p(jnp.exp(-jnp.abs(v)))
    c = _cumsum_rows(ls) * LOG2E
    hi = c.astype(BF16).astype(F32)
    r1 = c - hi
    mid = r1.astype(BF16).astype(F32)
    lo = (r1 - mid).astype(BF16).astype(F32)
    lane = lax.broadcasted_iota(jnp.int32, c.shape, 1)
    out = jnp.where(lane < N_HEADS, hi,
                    jnp.where(lane < 2 * N_HEADS, pltpu.roll(mid, N_HEADS, 1),
                              jnp.where(lane < C3_ONE, pltpu.roll(lo, 2 * N_HEADS, 1),
                                        jnp.where(lane == C3_ONE, 1.0, 0.0))))
    c3_ref[...] = out.astype(BF16)


def _fox_cum(fl, bias_pad, bsz, seq):
    return pl.pallas_call(
        _fox_cum_kernel,
        grid=(bsz,),
        in_specs=[pl.BlockSpec((seq, LANE), lambda b: (b, 0)), _full((1, LANE))],
        out_specs=pl.BlockSpec((seq, LANE), lambda b: (b, 0)),
        out_shape=jax.ShapeDtypeStruct(fl.shape, BF16),
        compiler_params=_params(("parallel",)),
        name="fox_cum",
    )(fl, bias_pad)


def _bias_placement():
    import numpy as np
    pq = np.zeros((LANE, QK_WIDE), np.float32)
    pk = np.zeros((LANE, QK_WIDE), np.float32)
    for h in range(N_HEADS):
        base = h * LANE + AUG0
        for piece in range(3):
            pq[piece * N_HEADS + h, base + piece] = 1.0
            pq[C3_ONE, base + 3 + piece] = 1.0
            pk[C3_ONE, base + piece] = 1.0
            pk[piece * N_HEADS + h, base + 3 + piece] = -1.0
    return jnp.asarray(pq, BF16), jnp.asarray(pk, BF16)


def _fox_attn_kernel(q_ref, k_ref, vt_ref, c3q_ref, c3k_ref, pq_ref, pk_ref, o_ref, ka_ref, sa_ref,
                     sb_ref, *, tq, tk, rows_k):
    qi = pl.program_id(1)
    seq = k_ref.shape[0]

    @pl.when(qi == 0)
    def _():
        def fill(i, _):
            r0 = pl.multiple_of(i * rows_k, rows_k)
            aug = jnp.dot(c3k_ref[pl.ds(r0, rows_k), :], pk_ref[...], preferred_element_type=F32)
            ka_ref[pl.ds(r0, rows_k), :] = (k_ref[pl.ds(r0, rows_k), :].astype(F32)
                                            + aug).astype(BF16)
            return 0
        lax.fori_loop(0, seq // rows_k, fill, 0)

    qa = (q_ref[...].astype(F32)
          + jnp.dot(c3q_ref[...], pq_ref[...], preferred_element_type=F32)).astype(BF16)
    krow = lax.broadcasted_iota(jnp.int32, (tk, tq), 0)
    qcol = lax.broadcasted_iota(jnp.int32, (tk, tq), 1)
    nkb = tq // tk
    nt = (((1,), (1,)), ((), ()))
    ones_rows = jnp.ones((ONES_ROWS, tk), BF16)

    def tree(fn, s):
        parts = [s[i * SUBLANE:(i + 1) * SUBLANE] for i in range(s.shape[0] // SUBLANE)]
        while len(parts) > 1:
            parts = [fn(parts[i], parts[i + 1]) for i in range(0, len(parts), 2)]
        return parts[0]

    hpb = sa_ref.shape[0]
    for j in range(N_HEADS // hpb):
        heads = tuple(range(j * hpb, (j + 1) * hpb))
        qh = tuple(qa[:, h * LANE:(h + 1) * LANE] for h in heads)

        def scores(buf, kb, heads=heads, qh=qh):
            r0 = pl.multiple_of(kb * tk, tk)
            for e, h in enumerate(heads):
                buf[e] = lax.dot_general(ka_ref[pl.ds(r0, tk), h * LANE:(h + 1) * LANE], qh[e], nt,
                                         preferred_element_type=F32)

        def absorb(buf, kb, stats, diag, nxt, heads=heads):
            if nxt is not None:
                scores(*nxt)
            new = []
            for e, h in enumerate(heads):
                m, acc = stats[e]
                se = buf[e]
                if diag is not None:
                    se = jnp.where(krow + diag * tk <= qcol, se, NEG_BIG)
                m_new = jnp.maximum(m, jnp.max(tree(jnp.maximum, se), axis=0, keepdims=True))
                p = jnp.exp2(se - m_new)
                alpha = jnp.exp2(m - m_new)
                vth = jnp.concatenate([vt_ref[kb, h * HEAD_DIM:(h + 1) * HEAD_DIM, :], ones_rows],
                                      axis=0)
                pv = jnp.dot(vth, p.astype(BF16), preferred_element_type=F32)
                new.append((m_new, alpha * acc + pv))
            return tuple(new)

        def step(t, stats):
            kb = nkb * t
            stats = absorb(sa_ref, kb, stats, None, (sb_ref, kb + 1))
            return absorb(sb_ref, kb + 1, stats, None, (sa_ref, kb + 2))

        init1 = (jnp.full((1, tq), NEG_BIG, F32), jnp.zeros((HEAD_DIM + ONES_ROWS, tq), F32))
        scores(sa_ref, 0)
        stats = lax.fori_loop(0, qi, step, (init1,) * hpb)
        kb = nkb * qi
        stats = absorb(sa_ref, kb, stats, 0, (sb_ref, kb + 1))
        stats = absorb(sb_ref, kb + 1, stats, 1, None)
        for e in range(0, hpb, 2):
            outs = []
            for _, acc in (stats[e], stats[e + 1]):
                outs.append(acc[:HEAD_DIM] / acc[HEAD_DIM:HEAD_DIM + 1])
            out_t = jnp.concatenate(outs, axis=0)
            c0 = (heads[e] // 2) * LANE
            o_ref[:, c0:c0 + LANE] = out_t.T.astype(o_ref.dtype)


def _fox_attn(q, k, vt, c3, pq, pk, bsz, seq, tk, hpb=4):
    m = q.shape[0]
    tq = 2 * tk
    nq = seq // tq
    vt4 = vt.reshape(bsz, seq // tk, WIDTH, tk)
    kernel = functools.partial(_fox_attn_kernel, tq=tq, tk=tk, rows_k=min(512, seq))
    return pl.pallas_call(
        kernel,
        grid=(bsz, nq),
        in_specs=[
            pl.BlockSpec((tq, QK_WIDE), lambda b, i: (b * nq + i, 0)),
            pl.BlockSpec((seq, QK_WIDE), lambda b, i: (b, 0)),
            pl.BlockSpec((None, seq // tk, WIDTH, tk), lambda b, i: (b, 0, 0, 0)),
            pl.BlockSpec((tq, LANE), lambda b, i: (b * nq + i, 0)),
            pl.BlockSpec((seq, LANE), lambda b, i: (b, 0)),
            _full(pq.shape), _full(pk.shape),
        ],
        out_specs=pl.BlockSpec((tq, WIDTH), lambda b, i: (b * nq + i, 0)),
        out_shape=jax.ShapeDtypeStruct((m, WIDTH), BF16),
        scratch_shapes=[pltpu.VMEM((seq, QK_WIDE), BF16), pltpu.VMEM((hpb, tk, tq), F32),
                        pltpu.VMEM((hpb, tk, tq), F32)],
        compiler_params=_params(("parallel", "arbitrary")),
        name="fox_attn",
    )(q, k, vt4, c3, c3, pq, pk)


def _ssd_kernel(xbc_ref, halo_ref, z_ref, dt_ref, cw_ref, cb_ref, dtb_ref, alog_ref, d_ref,
                nw_ref, o_ref, state_ref):
    c = pl.program_id(1)
    q = SSD_CHUNK

    @pl.when(c == 0)
    def _():
        state_ref[...] = jnp.zeros_like(state_ref)

    halo = jnp.where(c > 0, halo_ref[...], 0.0)
    full = jnp.concatenate([halo, xbc_ref[...]], axis=0)
    cw = cw_ref[...]
    acc = cb_ref[...]
    for k in range(SSD_CONV):
        off = SUBLANE - (SSD_CONV - 1) + k
        acc = acc + cw[k:k + 1, :] * full[off:off + q, :]
    xc = acc * _sigmoid(acc)
    xs = xc[:, :WIDTH]
    bm = xc[:, WIDTH:WIDTH + LANE]
    cm = xc[:, WIDTH + LANE:]

    dt = _softplus(dt_ref[...] + dtb_ref[...])
    adt = dt * (-jnp.exp(alog_ref[...]))
    a_cs = _cumsum_rows(adt)
    a_tot = a_cs[q - 1:q, :]
    xdt = xs * dt
    xw = (xdt * jnp.exp(a_tot - a_cs)).astype(BF16)
    dec_out = jnp.exp(a_cs)
    dec_tot = jnp.exp(a_tot)
    a_cs_t = a_cs.T
    xdt_b = xdt.astype(BF16)

    lo = lax.broadcasted_iota(jnp.int32, (1, LANE), 1) < SSD_STATE
    tril = (lax.broadcasted_iota(jnp.int32, (q, q), 0)
            >= lax.broadcasted_iota(jnp.int32, (q, q), 1))
    cm_g = (jnp.where(lo, cm, 0.0).astype(BF16), jnp.where(lo, 0.0, cm).astype(BF16))
    bm_g = (jnp.where(lo, bm, 0.0).astype(BF16), jnp.where(lo, 0.0, bm).astype(BF16))
    bm_b = bm.astype(BF16)
    nt = (((1,), (1,)), ((), ()))
    tn = (((0,), (0,)), ((), ()))
    cb = tuple(lax.dot_general(cm_g[g], bm_b, nt, preferred_element_type=F32) for g in (0, 1))

    ys = []
    for j in range(N_HEADS // 2):
        g = j // (N_HEADS // 2 // SSD_GROUPS)
        c0, c1 = j * LANE, (j + 1) * LANE
        yd = []
        for e in (0, 1):
            hl = (2 * j + e) * HEAD_DIM
            diff = a_cs[:, hl:hl + 1] - a_cs_t[hl:hl + 1, :]
            lm = jnp.exp(jnp.where(tril, diff, NEG_BIG))
            yd.append(jnp.dot((cb[g] * lm).astype(BF16), xdt_b[:, c0:c1],
                              preferred_element_type=F32))
        y_diag = jnp.where(lo, yd[0], yd[1])
        st = state_ref[j]
        y_off = jnp.dot(cm_g[g], st.astype(BF16), preferred_element_type=F32) * dec_out[:, c0:c1]
        state_ref[j] = dec_tot[:, c0:c1] * st + lax.dot_general(
            bm_g[g], xw[:, c0:c1], tn, preferred_element_type=F32)
        ys.append(y_diag + y_off)
    y = jnp.concatenate(ys, axis=1) + d_ref[...] * xs
    zf = z_ref[...]
    y = y * (zf * _sigmoid(zf))
    gw = WIDTH // SSD_GROUPS
    outs = []
    for g in range(SSD_GROUPS):
        yg = y[:, g * gw:(g + 1) * gw]
        outs.append(yg * lax.rsqrt(jnp.mean(yg * yg, axis=-1, keepdims=True) + RMS_EPS))
    o_ref[...] = (jnp.concatenate(outs, axis=1) * nw_ref[...]).astype(o_ref.dtype)


def _ssd(xbc, z, dt_rep, cw, cb, dtb_rep, alog_rep, d_rep, nw, bsz, seq):
    m = xbc.shape[0]
    q = SSD_CHUNK
    nc = seq // q
    row = lambda b, c: (b * nc + c, 0)
    halo = lambda b, c: (jnp.maximum((b * nc + c) * (q // SUBLANE) - 1, 0), 0)
    return pl.pallas_call(
        _ssd_kernel,
        grid=(bsz, nc),
        in_specs=[
            pl.BlockSpec((q, SSD_CONV_DIM), row),
            pl.BlockSpec((SUBLANE, SSD_CONV_DIM), halo),
            pl.BlockSpec((q, WIDTH), row),
            pl.BlockSpec((q, WIDTH), row),
            _full(cw.shape), _full(cb.shape), _full(dtb_rep.shape), _full(alog_rep.shape),
            _full(d_rep.shape), _full(nw.shape),
        ],
        out_specs=pl.BlockSpec((q, WIDTH), row),
        out_shape=jax.ShapeDtypeStruct((m, WIDTH), BF16),
        scratch_shapes=[pltpu.VMEM((N_HEADS // 2, LANE, LANE), F32)],
        compiler_params=_params(("parallel", "arbitrary")),
        name="ssd",
    )(xbc, xbc, z, dt_rep, cw, cb, dtb_rep, alog_rep, d_rep, nw)


def _s5_kernel(u_ref, wb_ref, wc_ref, are_ref, aim_ref, d_ref, wg_ref, bg_ref, o_ref,
               bu_ref, h_ref, *, steps, bsz):
    n = S5_LANES

    @pl.when(pl.program_id(0) == 0)
    def _():
        h_ref[...] = jnp.zeros_like(h_ref)

    u = pltpu.einshape("bth->tbh", u_ref[...]).reshape(steps * bsz, WIDTH)
    ub = u.astype(BF16)
    nslab = S5_GROUPS // S5_SLAB_GROUPS
    sw, sn = S5_SLAB_GROUPS * S5_GROUP, S5_SLAB_GROUPS * S5_STATE
    for c in range(nslab):
        for part in range(2):
            c0 = part * n + c * sn
            bu_ref[:, c0:c0 + sn] = jnp.dot(ub[:, c * sw:(c + 1) * sw],
                                            wb_ref[c * sw:(c + 1) * sw, c0:c0 + sn],
                                            preferred_element_type=F32)
    ar = are_ref[...]
    ai = aim_ref[...]

    def step(t, carry):
        hr, hi = carry
        r0 = pl.multiple_of(t * bsz, bsz)
        br = bu_ref[pl.ds(r0, bsz), 0:n]
        bi = bu_ref[pl.ds(r0, bsz), n:2 * n]
        nr = ar * hr - ai * hi + br
        ni = ar * hi + ai * hr + bi
        bu_ref[pl.ds(r0, bsz), 0:n] = nr
        bu_ref[pl.ds(r0, bsz), n:2 * n] = ni
        return nr, ni

    hr, hi = lax.fori_loop(0, steps, step, (h_ref[:, 0:n], h_ref[:, n:2 * n]))
    h_ref[:, 0:n] = hr
    h_ref[:, n:2 * n] = hi
    ys = []
    for c in range(nslab):
        yc = None
        for part in range(2):
            c0 = part * n + c * sn
            t = jnp.dot(bu_ref[:, c0:c0 + sn].astype(BF16), wc_ref[c0:c0 + sn, c * sw:(c + 1) * sw],
                        preferred_element_type=F32)
            yc = t if yc is None else yc + t
        ys.append(yc)
    y = jnp.concatenate(ys, axis=1) + d_ref[...] * u
    gy = jax.nn.gelu(y)
    gl = jnp.dot(gy.astype(BF16), wg_ref[...], preferred_element_type=F32) + bg_ref[...]
    out = (gy * _sigmoid(gl)).reshape(steps, bsz, WIDTH)
    o_ref[...] = pltpu.einshape("tbh->bth", out).astype(o_ref.dtype)


def _s5(u3, wb, wc, a_re, a_im, d, wg, bg, bsz, seq, steps=64):
    rows = steps * bsz
    kernel = functools.partial(_s5_kernel, steps=steps, bsz=bsz)
    return pl.pallas_call(
        kernel,
        grid=(seq // steps,),
        in_specs=[
            pl.BlockSpec((bsz, steps, WIDTH), lambda i: (0, i, 0)),
            _full(wb.shape), _full(wc.shape), _full(a_re.shape), _full(a_im.shape),
            _full(d.shape), _full(wg.shape), _full(bg.shape),
        ],
        out_specs=pl.BlockSpec((bsz, steps, WIDTH), lambda i: (0, i, 0)),
        out_shape=jax.ShapeDtypeStruct(u3.shape, BF16),
        scratch_shapes=[pltpu.VMEM((rows, 2 * S5_LANES), F32),
                        pltpu.VMEM((bsz, 2 * S5_LANES), F32)],
        compiler_params=_params(("arbitrary",)),
        name="s5",
    )(u3, wb, wc, a_re, a_im, d, wg, bg)


def _s5_weights(a_re, a_im, b_re, b_im, c_re, c_im, log_step, bsz):
    lam_re = jnp.minimum(a_re, -1e-4)
    lam_im = a_im
    step = jnp.exp(log_step)[:, None]
    mag = jnp.exp(lam_re * step)
    abar_re = mag * jnp.cos(lam_im * step)
    abar_im = mag * jnp.sin(lam_im * step)
    den = lam_re * lam_re + lam_im * lam_im
    num_re = abar_re - 1.0
    k_re = (num_re * lam_re + abar_im * lam_im) / den
    k_im = (abar_im * lam_re - num_re * lam_im) / den
    bb_re = k_re[..., None] * b_re - k_im[..., None] * b_im
    bb_im = k_re[..., None] * b_im + k_im[..., None] * b_re
    eye = jnp.eye(S5_GROUPS, dtype=F32)

    def in_blocks(bb):
        t = jnp.transpose(bb, (0, 2, 1))[:, :, None, :] * eye[:, None, :, None]
        return t.reshape(WIDTH, S5_LANES)

    def out_blocks(cc):
        t = jnp.transpose(cc, (0, 2, 1))[:, :, None, :] * eye[:, None, :, None]
        return t.reshape(S5_LANES, WIDTH)

    wb = jnp.concatenate([in_blocks(bb_re), in_blocks(bb_im)], axis=1).astype(BF16)
    wc = jnp.concatenate([out_blocks(c_re), -out_blocks(c_im)], axis=0).astype(BF16)
    rep = lambda a: jnp.broadcast_to(a.reshape(1, S5_LANES), (bsz, S5_LANES))
    return wb, wc, rep(abar_re), rep(abar_im)


def _merge_kernel(ya_ref, yb_ref, yc_ref, gate_ref, x_ref, wbr_ref, bg_ref, wo_ref, g_ref, b_ref,
                  o_ref, *, alpha):
    merged = None
    for r, y_ref in enumerate((ya_ref, yb_ref, yc_ref)):
        pr = jnp.dot(y_ref[...], wbr_ref[r], preferred_element_type=F32)
        gt = _sigmoid(gate_ref[:, r * D_MODEL:(r + 1) * D_MODEL] + bg_ref[r:r + 1, :])
        merged = gt * pr if merged is None else merged + gt * pr
    mix = jnp.dot(merged.astype(BF16), wo_ref[...], preferred_element_type=F32)
    o_ref[...] = _layer_norm(alpha * x_ref[...] + mix, g_ref[...], b_ref[...])


def _merge(ya, yb, yc, gates, x2, wbr, bg, wo, g, b, alpha, tm=512):
    m = x2.shape[0]
    row = lambda i: (i, 0)
    kernel = functools.partial(_merge_kernel, alpha=alpha)
    return pl.pallas_call(
        kernel,
        grid=(m // tm,),
        in_specs=[
            pl.BlockSpec((tm, WIDTH), row),
            pl.BlockSpec((tm, WIDTH), row),
            pl.BlockSpec((tm, WIDTH), row),
            pl.BlockSpec((tm, N_BRANCH * D_MODEL), row),
            pl.BlockSpec((tm, D_MODEL), row),
            _full(wbr.shape), _full(bg.shape), _full(wo.shape), _full(g.shape), _full(b.shape),
        ],
        out_specs=pl.BlockSpec((tm, D_MODEL), row),
        out_shape=jax.ShapeDtypeStruct((m, D_MODEL), F32),
        compiler_params=_params(("parallel",)),
        name="merge_ln1",
    )(ya, yb, yc, gates, x2, wbr, bg, wo, g, b)


def _ffn_kernel(x_ref, halo_ref, wu_ref, cw_ref, cb_ref, wd_ref, g_ref, b_ref, o_ref, *,
                alpha, nt, tf, rb):
    tm = x_ref.shape[0]
    first = (pl.program_id(0) % nt) == 0
    x = x_ref[...]
    halo = jnp.where(first, 0.0, halo_ref[...])
    xf = jnp.concatenate([halo, x], axis=0)
    nrb = tm // rb
    xs = [xf[(0 if r == 0 else SUBLANE + r * rb):SUBLANE + (r + 1) * rb].astype(BF16)
          for r in range(nrb)]

    def up(j, r):
        cv, cg = j * tf, D_FF + j * tf
        return (jnp.dot(xs[r], wu_ref[:, cv:cv + tf], preferred_element_type=F32),
                jnp.dot(xs[r], wu_ref[:, cg:cg + tf], preferred_element_type=F32))

    def conv(h, c0, r):
        acc = cb_ref[:, c0:c0 + tf]
        for k in range(FFN_CONV):
            off = SUBLANE - (FFN_CONV - 1) + k + r * rb
            acc = acc + cw_ref[k:k + 1, c0:c0 + tf] * h[off:off + rb, :]
        return acc

    nch = D_FF // tf
    outs = [None] * nrb
    pieces = [up(0, r) for r in range(nrb)]
    for j in range(nch):
        cv, cg = j * tf, D_FF + j * tf
        hv = jnp.concatenate([p[0] for p in pieces], axis=0)
        hg = jnp.concatenate([p[1] for p in pieces], axis=0)
        nxt = []
        for r in range(nrb):
            if j + 1 < nch:
                nxt.append(up(j + 1, r))
            val = conv(hv, cv, r)
            gate = conv(hg, cg, r)
            act = (gate * _sigmoid(gate) * val).astype(BF16)
            part = jnp.dot(act, wd_ref[cv:cv + tf, :], preferred_element_type=F32)
            outs[r] = part if outs[r] is None else outs[r] + part
        pieces = nxt
    out = jnp.concatenate(outs, axis=0)
    o_ref[...] = _layer_norm(alpha * x + out, g_ref[...], b_ref[...])


def _ffn(x2, wu, cw, cb, wd, g, b, seq, alpha, tm=1024, tf=256, rb=256):
    m = x2.shape[0]
    nt = seq // tm
    row = lambda i: (i, 0)
    kernel = functools.partial(_ffn_kernel, alpha=alpha, nt=nt, tf=tf, rb=rb)
    return pl.pallas_call(
        kernel,
        grid=(m // tm,),
        in_specs=[
            pl.BlockSpec((tm, D_MODEL), row),
            pl.BlockSpec((SUBLANE, D_MODEL), lambda i: (jnp.maximum(i * (tm // SUBLANE) - 1, 0), 0)),
            _full(wu.shape), _full(cw.shape), _full(cb.shape), _full(wd.shape),
            _full(g.shape), _full(b.shape),
        ],
        out_specs=pl.BlockSpec((tm, D_MODEL), row),
        out_shape=jax.ShapeDtypeStruct((m, D_MODEL), F32),
        compiler_params=_params(("parallel",)),
        name="ffn_ln2",
    )(x2, x2, wu, cw, cb, wd, g, b)


def _in_proj_weight(w_in):
    o = 0
    parts = {}
    for name, size in (("q", WIDTH), ("k", WIDTH), ("v", WIDTH), ("f", N_HEADS), ("z", WIDTH),
                       ("xbc", SSD_CONV_DIM), ("dt", N_HEADS), ("u", WIDTH),
                       ("gate", N_BRANCH * D_MODEL)):
        parts[name] = w_in[:, o:o + size]
        o += size

    def wide(w):
        w = w.reshape(D_MODEL, N_HEADS, HEAD_DIM)
        return jnp.pad(w, ((0, 0), (0, 0), (0, LANE - HEAD_DIM))).reshape(D_MODEL, QK_WIDE)

    dt_rep = jnp.repeat(parts["dt"], HEAD_DIM, axis=1)
    f_pad = jnp.pad(parts["f"], ((0, 0), (0, LANE - N_HEADS)))
    q_scale = HEAD_DIM ** -0.5 * LOG2E
    w = jnp.concatenate([wide(parts["q"] * q_scale), wide(parts["k"]), parts["xbc"], parts["z"],
                         dt_rep, parts["u"], parts["gate"], f_pad], axis=1)
    return w.astype(BF16), parts["v"].T.astype(BF16)


def kernel(x, w_in, fox_f_bias, ssd_conv_w, ssd_conv_b, ssd_dt_bias, ssd_a_log, ssd_d, ssd_norm_w, s5_a_re, s5_a_im, s5_b_re, s5_b_im, s5_c_re, s5_c_im, s5_log_step, s5_d, s5_w_glu, s5_b_glu, w_branch, b_gate, w_out, ln1_g, ln1_b, ffn_w_up, ffn_conv_w, ffn_conv_b, ffn_w_down, ln2_g, ln2_b):
    bsz, seq, _ = x.shape
    depth = w_in.shape[0]
    alpha = (2 * depth) ** 0.25
    tk = 256
    pq, pk = _bias_placement()
    rep = lambda a: jnp.repeat(a, HEAD_DIM).reshape(1, WIDTH)
    row = lambda a: a.reshape(1, -1)
    x2 = x.reshape(bsz * seq, D_MODEL)
    for i in range(depth):
        w, wvt = _in_proj_weight(w_in[i])
        q, k, vt, xbc, z, dt_rep, u, gates, fl = _in_proj(x2, w, wvt, tk)
        f_bias = jnp.pad(fox_f_bias[i], (0, LANE - N_HEADS)).reshape(1, LANE)
        c3 = _fox_cum(fl, f_bias, bsz, seq)
        ya = _fox_attn(q, k, vt, c3, pq, pk, bsz, seq, tk)
        yb = _ssd(xbc, z, dt_rep, ssd_conv_w[i], row(ssd_conv_b[i]), rep(ssd_dt_bias[i]),
                  rep(ssd_a_log[i]), rep(ssd_d[i]), row(ssd_norm_w[i]), bsz, seq)
        wb, wc, a_re, a_im = _s5_weights(s5_a_re[i], s5_a_im[i], s5_b_re[i], s5_b_im[i],
                                         s5_c_re[i], s5_c_im[i], s5_log_step[i], bsz)
        yc = _s5(u.reshape(bsz, seq, WIDTH), wb, wc, a_re, a_im, row(s5_d[i]),
                 s5_w_glu[i].astype(BF16), row(s5_b_glu[i]), bsz, seq)
        x2 = _merge(ya, yb, yc.reshape(bsz * seq, WIDTH), gates, x2, w_branch[i].astype(BF16),
                    b_gate[i], w_out[i].astype(BF16), row(ln1_g[i]), row(ln1_b[i]), alpha)
        x2 = _ffn(x2, ffn_w_up[i].astype(BF16), ffn_conv_w[i], row(ffn_conv_b[i]),
                  ffn_w_down[i].astype(BF16), row(ln2_g[i]), row(ln2_b[i]), seq, alpha)
    return x2.reshape(bsz, seq, D_MODEL)
```

```python
import functools
import math

import jax
import jax.numpy as jnp
from jax import lax
from jax.experimental import pallas as pl
from jax.experimental.pallas import tpu as pltpu

F32 = jnp.float32
BF16 = jnp.bfloat16

D_MODEL = 1024
N_HEADS = 8
HEAD_DIM = 64
WIDTH = N_HEADS * HEAD_DIM
SSD_STATE = 64
SSD_GROUPS = 2
SSD_CONV = 4
SSD_CHUNK = 128
SSD_CONV_DIM = WIDTH + 2 * SSD_GROUPS * SSD_STATE
S5_GROUP = 16
S5_GROUPS = WIDTH // S5_GROUP
S5_STATE = 64
S5_LANES = S5_GROUPS * S5_STATE
S5_SLAB_GROUPS = 16
D_FF = 2816
FFN_CONV = 3
N_BRANCH = 3
LN_EPS = 1e-5
RMS_EPS = 1e-5
LANE = 128
SUBLANE = 8
NEG_BIG = -1e30
VMEM_LIMIT = 56 * 1024 * 1024

QK_WIDE = N_HEADS * LANE
AUG0 = HEAD_DIM
C3_ONE = 3 * N_HEADS
LOG2E = math.log2(math.e)
ONES_ROWS = 16

_C_Q = 0
_C_K = _C_Q + WIDTH
_C_XBC = _C_K + WIDTH
_C_Z = _C_XBC + SSD_CONV_DIM
_C_U = _C_Z + WIDTH
_C_GATE = _C_U + WIDTH
_C_FL = _C_GATE + N_BRANCH * D_MODEL
_C_END = _C_FL + LANE


def _params(sem):
    return pltpu.CompilerParams(dimension_semantics=sem, vmem_limit_bytes=VMEM_LIMIT)


def _full(shape):
    n = len(shape)
    return pl.BlockSpec(shape, lambda *_: (0,) * n, pipeline_mode=pl.Buffered(1))


def _sigmoid(x):
    return 1.0 / (1.0 + jnp.exp(-x))


def _softplus(x):
    return jnp.maximum(x, 0.0) + jnp.log1p(jnp.exp(-jnp.abs(x)))


def _cumsum_rows(x):
    n = x.shape[0]
    row = lax.broadcasted_iota(jnp.int32, x.shape, 0)
    s = 1
    while s < n:
        x = x + jnp.where(row >= s, pltpu.roll(x, s, 0), 0.0)
        s *= 2
    return x


def _layer_norm(v, g, b):
    mu = jnp.mean(v, axis=-1, keepdims=True)
    d = v - mu
    var = jnp.mean(d * d, axis=-1, keepdims=True)
    return d * lax.rsqrt(var + LN_EPS) * g + b


def _in_proj_kernel(x_ref, w_ref, wvt_ref, q_ref, k_ref, vt_ref, xbc_ref, z_ref, u_ref,
                    gate_ref, fl_ref):
    xb = x_ref[...].astype(BF16)
    tk = vt_ref.shape[2]

    def mm(c0, c1):
        return jnp.dot(xb, w_ref[:, c0:c1], preferred_element_type=F32)

    lo = lax.broadcasted_iota(jnp.int32, (1, LANE), 1) < HEAD_DIM

    def widen(y):
        blocks = []
        for j in range(N_HEADS // 2):
            blk = y[:, j * LANE:(j + 1) * LANE]
            blocks.append(jnp.where(lo, blk, 0.0))
            blocks.append(jnp.where(lo, pltpu.roll(blk, HEAD_DIM, 1), 0.0))
        return jnp.concatenate(blocks, axis=1)

    q_ref[...] = widen(mm(_C_Q, _C_K)).astype(BF16)
    k_ref[...] = widen(mm(_C_K, _C_XBC)).astype(BF16)
    for r in range(vt_ref.shape[0]):
        vt_ref[r] = lax.dot_general(wvt_ref[...], xb[r * tk:(r + 1) * tk], (((1,), (1,)), ((), ())),
                                    preferred_element_type=F32).astype(BF16)
    xbc_ref[...] = mm(_C_XBC, _C_Z)
    z_ref[...] = mm(_C_Z, _C_U)
    u_ref[...] = mm(_C_U, _C_GATE)
    gate_ref[...] = mm(_C_GATE, _C_FL).astype(BF16)
    fl_ref[...] = mm(_C_FL, _C_END)


def _in_proj(x2, w, wvt, tk, tm=512):
    m = x2.shape[0]
    row = lambda i: (i, 0)
    out_shape = (
        jax.ShapeDtypeStruct((m, QK_WIDE), BF16),
        jax.ShapeDtypeStruct((m, QK_WIDE), BF16),
        jax.ShapeDtypeStruct((m // tk, WIDTH, tk), BF16),
        jax.ShapeDtypeStruct((m, SSD_CONV_DIM), F32),
        jax.ShapeDtypeStruct((m, WIDTH), F32),
        jax.ShapeDtypeStruct((m, WIDTH), F32),
        jax.ShapeDtypeStruct((m, N_BRANCH * D_MODEL), BF16),
        jax.ShapeDtypeStruct((m, LANE), F32),
    )
    out_specs = (
        pl.BlockSpec((tm, QK_WIDE), row),
        pl.BlockSpec((tm, QK_WIDE), row),
        pl.BlockSpec((tm // tk, WIDTH, tk), lambda i: (i, 0, 0)),
        pl.BlockSpec((tm, SSD_CONV_DIM), row),
        pl.BlockSpec((tm, WIDTH), row),
        pl.BlockSpec((tm, WIDTH), row),
        pl.BlockSpec((tm, N_BRANCH * D_MODEL), row),
        pl.BlockSpec((tm, LANE), row),
    )
    return pl.pallas_call(
        _in_proj_kernel,
        grid=(m // tm,),
        in_specs=[pl.BlockSpec((tm, D_MODEL), row), _full(w.shape), _full(wvt.shape)],
        out_specs=out_specs,
        out_shape=out_shape,
        compiler_params=_params(("parallel",)),
        name="in_proj",
    )(x2, w, wvt)


def _fox_cum_kernel(fl_ref, bias_ref, c3_ref):
    v = fl_ref[...] + bias_ref[...]
    ls = jnp.minimum(v, 0.0) - jnp.log1p(jnp.exp(-jnp.abs(v)))
    c = _cumsum_rows(ls) * LOG2E
    hi = c.astype(BF16).astype(F32)
    r1 = c - hi
    mid = r1.astype(BF16).astype(F32)
    lo = (r1 - mid).astype(BF16).astype(F32)
    lane = lax.broadcasted_iota(jnp.int32, c.shape, 1)
    out = jnp.where(lane < N_HEADS, hi,
                    jnp.where(lane < 2 * N_HEADS, pltpu.roll(mid, N_HEADS, 1),
                              jnp.where(lane < C3_ONE, pltpu.roll(lo, 2 * N_HEADS, 1),
                                        jnp.where(lane == C3_ONE, 1.0, 0.0))))
    c3_ref[...] = out.astype(BF16)


def _fox_cum(fl, bias_pad, bsz, seq):
    return pl.pallas_call(
        _fox_cum_kernel,
        grid=(bsz,),
        in_specs=[pl.BlockSpec((seq, LANE), lambda b: (b, 0)), _full((1, LANE))],
        out_specs=pl.BlockSpec((seq, LANE), lambda b: (b, 0)),
        out_shape=jax.ShapeDtypeStruct(fl.shape, BF16),
        compiler_params=_params(("parallel",)),
        name="fox_cum",
    )(fl, bias_pad)


def _bias_placement():
    import numpy as np
    pq = np.zeros((LANE, QK_WIDE), np.float32)
    pk = np.zeros((LANE, QK_WIDE), np.float32)
    for h in range(N_HEADS):
        base = h * LANE + AUG0
        for piece in range(3):
            pq[piece * N_HEADS + h, base + piece] = 1.0
            pq[C3_ONE, base + 3 + piece] = 1.0
            pk[C3_ONE, base + piece] = 1.0
            pk[piece * N_HEADS + h, base + 3 + piece] = -1.0
    return jnp.asarray(pq, BF16), jnp.asarray(pk, BF16)


def _fox_attn_kernel(q_ref, k_ref, vt_ref, c3q_ref, c3k_ref, pq_ref, pk_ref, o_ref, ka_ref, sa_ref,
                     sb_ref, *, tq, tk, rows_k):
    qi = pl.program_id(1)
    seq = k_ref.shape[0]

    @pl.when(qi == 0)
    def _():
        def fill(i, _):
            r0 = pl.multiple_of(i * rows_k, rows_k)
            aug = jnp.dot(c3k_ref[pl.ds(r0, rows_k), :], pk_ref[...], preferred_element_type=F32)
            ka_ref[pl.ds(r0, rows_k), :] = (k_ref[pl.ds(r0, rows_k), :].astype(F32)
                                            + aug).astype(BF16)
            return 0
        lax.fori_loop(0, seq // rows_k, fill, 0)

    qa = (q_ref[...].astype(F32)
          + jnp.dot(c3q_ref[...], pq_ref[...], preferred_element_type=F32)).astype(BF16)
    krow = lax.broadcasted_iota(jnp.int32, (tk, tq), 0)
    qcol = lax.broadcasted_iota(jnp.int32, (tk, tq), 1)
    nkb = tq // tk
    nt = (((1,), (1,)), ((), ()))
    ones_rows = jnp.ones((ONES_ROWS, tk), BF16)

    def tree(fn, s):
        parts = [s[i * SUBLANE:(i + 1) * SUBLANE] for i in range(s.shape[0] // SUBLANE)]
        while len(parts) > 1:
            parts = [fn(parts[i], parts[i + 1]) for i in range(0, len(parts), 2)]
        return parts[0]

    hpb = sa_ref.shape[0]
    for j in range(N_HEADS // hpb):
        heads = tuple(range(j * hpb, (j + 1) * hpb))
        qh = tuple(qa[:, h * LANE:(h + 1) * LANE] for h in heads)

        def scores(buf, kb, heads=heads, qh=qh):
            r0 = pl.multiple_of(kb * tk, tk)
            for e, h in enumerate(heads):
                buf[e] = lax.dot_general(ka_ref[pl.ds(r0, tk), h * LANE:(h + 1) * LANE], qh[e], nt,
                                         preferred_element_type=F32)

        def absorb(buf, kb, stats, diag, nxt, heads=heads):
            if nxt is not None:
                scores(*nxt)
            new = []
            for e, h in enumerate(heads):
                m, acc = stats[e]
                se = buf[e]
                if diag is not None:
                    se = jnp.where(krow + diag * tk <= qcol, se, NEG_BIG)
                m_new = jnp.maximum(m, jnp.max(tree(jnp.maximum, se), axis=0, keepdims=True))
                p = jnp.exp2(se - m_new)
                alpha = jnp.exp2(m - m_new)
                vth = jnp.concatenate([vt_ref[kb, h * HEAD_DIM:(h + 1) * HEAD_DIM, :], ones_rows],
                                      axis=0)
                pv = jnp.dot(vth, p.astype(BF16), preferred_element_type=F32)
                new.append((m_new, alpha * acc + pv))
            return tuple(new)

        def step(t, stats):
            kb = nkb * t
            stats = absorb(sa_ref, kb, stats, None, (sb_ref, kb + 1))
            return absorb(sb_ref, kb + 1, stats, None, (sa_ref, kb + 2))

        init1 = (jnp.full((1, tq), NEG_BIG, F32), jnp.zeros((HEAD_DIM + ONES_ROWS, tq), F32))
        scores(sa_ref, 0)
        stats = lax.fori_loop(0, qi, step, (init1,) * hpb)
        kb = nkb * qi
        stats = absorb(sa_ref, kb, stats, 0, (sb_ref, kb + 1))
        stats = absorb(sb_ref, kb + 1, stats, 1, None)
        for e in range(0, hpb, 2):
            outs = []
            for _, acc in (stats[e], stats[e + 1]):
                outs.append(acc[:HEAD_DIM] / acc[HEAD_DIM:HEAD_DIM + 1])
            out_t = jnp.concatenate(outs, axis=0)
            c0 = (heads[e] // 2) * LANE
            o_ref[:, c0:c0 + LANE] = out_t.T.astype(o_ref.dtype)


def _fox_attn(q, k, vt, c3, pq, pk, bsz, seq, tk, hpb=4):
    m = q.shape[0]
    tq = 2 * tk
    nq = seq // tq
    vt4 = vt.reshape(bsz, seq // tk, WIDTH, tk)
    kernel = functools.partial(_fox_attn_kernel, tq=tq, tk=tk, rows_k=min(512, seq))
    return pl.pallas_call(
        kernel,
        grid=(bsz, nq),
        in_specs=[
            pl.BlockSpec((tq, QK_WIDE), lambda b, i: (b * nq + i, 0)),
            pl.BlockSpec((seq, QK_WIDE), lambda b, i: (b, 0)),
            pl.BlockSpec((None, seq // tk, WIDTH, tk), lambda b, i: (b, 0, 0, 0)),
            pl.BlockSpec((tq, LANE), lambda b, i: (b * nq + i, 0)),
            pl.BlockSpec((seq, LANE), lambda b, i: (b, 0)),
            _full(pq.shape), _full(pk.shape),
        ],
        out_specs=pl.BlockSpec((tq, WIDTH), lambda b, i: (b * nq + i, 0)),
        out_shape=jax.ShapeDtypeStruct((m, WIDTH), BF16),
        scratch_shapes=[pltpu.VMEM((seq, QK_WIDE), BF16), pltpu.VMEM((hpb, tk, tq), F32),
                        pltpu.VMEM((hpb, tk, tq), F32)],
        compiler_params=_params(("parallel", "arbitrary")),
        name="fox_attn",
    )(q, k, vt4, c3, c3, pq, pk)


def _ssd_kernel(xbc_ref, halo_ref, z_ref, fl_ref, erep_ref, cw_ref, cb_ref, dtb_ref, alog_ref, d_ref,
                nw_ref, o_ref, state_ref):
    c = pl.program_id(1)
    q = SSD_CHUNK

    @pl.when(c == 0)
    def _():
        state_ref[...] = jnp.zeros_like(state_ref)

    halo = jnp.where(c > 0, halo_ref[...], 0.0)
    full = jnp.concatenate([halo, xbc_ref[...]], axis=0)
    cw = cw_ref[...]
    acc = cb_ref[...]
    for k in range(SSD_CONV):
        off = SUBLANE - (SSD_CONV - 1) + k
        acc = acc + cw[k:k + 1, :] * full[off:off + q, :]
    xc = acc * _sigmoid(acc)
    xs = xc[:, :WIDTH]
    bm = xc[:, WIDTH:WIDTH + LANE]
    cm = xc[:, WIDTH + LANE:]

    raw = fl_ref[...]
    hi = raw.astype(BF16)
    r1 = raw - hi.astype(F32)
    mid = r1.astype(BF16)
    low = (r1 - mid.astype(F32)).astype(BF16)
    erep = erep_ref[...]
    dt_raw = (jnp.dot(hi, erep, preferred_element_type=F32)
              + jnp.dot(mid, erep, preferred_element_type=F32)
              + jnp.dot(low, erep, preferred_element_type=F32))
    dt = _softplus(dt_raw + dtb_ref[...])
    adt = dt * (-jnp.exp(alog_ref[...]))
    a_cs = _cumsum_rows(adt)
    a_tot = a_cs[q - 1:q, :]
    xdt = xs * dt
    xw = (xdt * jnp.exp(a_tot - a_cs)).astype(BF16)
    dec_out = jnp.exp(a_cs)
    dec_tot = jnp.exp(a_tot)
    a_cs_t = a_cs.T
    xdt_b = xdt.astype(BF16)

    lo = lax.broadcasted_iota(jnp.int32, (1, LANE), 1) < SSD_STATE
    tril = (lax.broadcasted_iota(jnp.int32, (q, q), 0)
            >= lax.broadcasted_iota(jnp.int32, (q, q), 1))
    cm_g = (jnp.where(lo, cm, 0.0).astype(BF16), jnp.where(lo, 0.0, cm).astype(BF16))
    bm_g = (jnp.where(lo, bm, 0.0).astype(BF16), jnp.where(lo, 0.0, bm).astype(BF16))
    bm_b = bm.astype(BF16)
    nt = (((1,), (1,)), ((), ()))
    tn = (((0,), (0,)), ((), ()))
    cb = tuple(lax.dot_general(cm_g[g], bm_b, nt, preferred_element_type=F32) for g in (0, 1))

    ys = []
    for j in range(N_HEADS // 2):
        g = j // (N_HEADS // 2 // SSD_GROUPS)
        c0, c1 = j * LANE, (j + 1) * LANE
        yd = []
        for e in (0, 1):
            hl = (2 * j + e) * HEAD_DIM
            diff = a_cs[:, hl:hl + 1] - a_cs_t[hl:hl + 1, :]
            lm = jnp.exp(jnp.where(tril, diff, NEG_BIG))
            yd.append(jnp.dot((cb[g] * lm).astype(BF16), xdt_b[:, c0:c1],
                              preferred_element_type=F32))
        y_diag = jnp.where(lo, yd[0], yd[1])
        st = state_ref[j]
        y_off = jnp.dot(cm_g[g], st.astype(BF16), preferred_element_type=F32) * dec_out[:, c0:c1]
        state_ref[j] = dec_tot[:, c0:c1] * st + lax.dot_general(
            bm_g[g], xw[:, c0:c1], tn, preferred_element_type=F32)
        ys.append(y_diag + y_off)
    y = jnp.concatenate(ys, axis=1) + d_ref[...] * xs
    zf = z_ref[...]
    y = y * (zf * _sigmoid(zf))
    gw = WIDTH // SSD_GROUPS
    outs = []
    for g in range(SSD_GROUPS):
        yg = y[:, g * gw:(g + 1) * gw]
        outs.append(yg * lax.rsqrt(jnp.mean(yg * yg, axis=-1, keepdims=True) + RMS_EPS))
    o_ref[...] = (jnp.concatenate(outs, axis=1) * nw_ref[...]).astype(o_ref.dtype)


def _head_repeat_matrix():
    import numpy as np
    e = np.zeros((LANE, WIDTH), np.float32)
    for h in range(N_HEADS):
        e[N_HEADS + h, h * HEAD_DIM:(h + 1) * HEAD_DIM] = 1.0
    return jnp.asarray(e, BF16)


def _ssd(xbc, z, fl, erep, cw, cb, dtb_rep, alog_rep, d_rep, nw, bsz, seq):
    m = xbc.shape[0]
    q = SSD_CHUNK
    nc = seq // q
    row = lambda b, c: (b * nc + c, 0)
    halo = lambda b, c: (jnp.maximum((b * nc + c) * (q // SUBLANE) - 1, 0), 0)
    return pl.pallas_call(
        _ssd_kernel,
        grid=(bsz, nc),
        in_specs=[
            pl.BlockSpec((q, SSD_CONV_DIM), row),
            pl.BlockSpec((SUBLANE, SSD_CONV_DIM), halo),
            pl.BlockSpec((q, WIDTH), row),
            pl.BlockSpec((q, LANE), row),
            _full(erep.shape),
            _full(cw.shape), _full(cb.shape), _full(dtb_rep.shape), _full(alog_rep.shape),
            _full(d_rep.shape), _full(nw.shape),
        ],
        out_specs=pl.BlockSpec((q, WIDTH), row),
        out_shape=jax.ShapeDtypeStruct((m, WIDTH), BF16),
        scratch_shapes=[pltpu.VMEM((N_HEADS // 2, LANE, LANE), F32)],
        compiler_params=_params(("parallel", "arbitrary")),
        name="ssd",
    )(xbc, xbc, z, fl, erep, cw, cb, dtb_rep, alog_rep, d_rep, nw)


def _s5_kernel(u_ref, wb_ref, wc_ref, are_ref, aim_ref, d_ref, wg_ref, bg_ref, o_ref,
               bu_ref, h_ref, *, steps, bsz):
    n = S5_LANES

    @pl.when(pl.program_id(0) == 0)
    def _():
        h_ref[...] = jnp.zeros_like(h_ref)

    u = pltpu.einshape("bth->tbh", u_ref[...]).reshape(steps * bsz, WIDTH)
    ub = u.astype(BF16)
    nslab = S5_GROUPS // S5_SLAB_GROUPS
    sw, sn = S5_SLAB_GROUPS * S5_GROUP, S5_SLAB_GROUPS * S5_STATE
    for c in range(nslab):
        for part in range(2):
            c0 = part * n + c * sn
            bu_ref[:, c0:c0 + sn] = jnp.dot(ub[:, c * sw:(c + 1) * sw],
                                            wb_ref[c * sw:(c + 1) * sw, c0:c0 + sn],
                                            preferred_element_type=F32)
    ar = are_ref[...]
    ai = aim_ref[...]

    def step(t, carry):
        hr, hi = carry
        r0 = pl.multiple_of(t * bsz, bsz)
        br = bu_ref[pl.ds(r0, bsz), 0:n]
        bi = bu_ref[pl.ds(r0, bsz), n:2 * n]
        nr = ar * hr - ai * hi + br
        ni = ar * hi + ai * hr + bi
        bu_ref[pl.ds(r0, bsz), 0:n] = nr
        bu_ref[pl.ds(r0, bsz), n:2 * n] = ni
        return nr, ni

    hr, hi = lax.fori_loop(0, steps, step, (h_ref[:, 0:n], h_ref[:, n:2 * n]))
    h_ref[:, 0:n] = hr
    h_ref[:, n:2 * n] = hi
    ys = []
    for c in range(nslab):
        yc = None
        for part in range(2):
            c0 = part * n + c * sn
            t = jnp.dot(bu_ref[:, c0:c0 + sn].astype(BF16), wc_ref[c0:c0 + sn, c * sw:(c + 1) * sw],
                        preferred_element_type=F32)
            yc = t if yc is None else yc + t
        ys.append(yc)
    y = jnp.concatenate(ys, axis=1) + d_ref[...] * u
    gy = jax.nn.gelu(y)
    gl = jnp.dot(gy.astype(BF16), wg_ref[...], preferred_element_type=F32) + bg_ref[...]
    out = (gy * _sigmoid(gl)).reshape(steps, bsz, WIDTH)
    o_ref[...] = pltpu.einshape("tbh->bth", out).astype(o_ref.dtype)


def _s5(u3, wb, wc, a_re, a_im, d, wg, bg, bsz, seq, steps=64):
    rows = steps * bsz
    kernel = functools.partial(_s5_kernel, steps=steps, bsz=bsz)
    return pl.pallas_call(
        kernel,
        grid=(seq // steps,),
        in_specs=[
            pl.BlockSpec((bsz, steps, WIDTH), lambda i: (0, i, 0)),
            _full(wb.shape), _full(wc.shape), _full(a_re.shape), _full(a_im.shape),
            _full(d.shape), _full(wg.shape), _full(bg.shape),
        ],
        out_specs=pl.BlockSpec((bsz, steps, WIDTH), lambda i: (0, i, 0)),
        out_shape=jax.ShapeDtypeStruct(u3.shape, BF16),
        scratch_shapes=[pltpu.VMEM((rows, 2 * S5_LANES), F32),
                        pltpu.VMEM((bsz, 2 * S5_LANES), F32)],
        compiler_params=_params(("arbitrary",)),
        name="s5",
    )(u3, wb, wc, a_re, a_im, d, wg, bg)


def _s5_weights(a_re, a_im, b_re, b_im, c_re, c_im, log_step, bsz):
    lam_re = jnp.minimum(a_re, -1e-4)
    lam_im = a_im
    step = jnp.exp(log_step)[:, None]
    mag = jnp.exp(lam_re * step)
    abar_re = mag * jnp.cos(lam_im * step)
    abar_im = mag * jnp.sin(lam_im * step)
    den = lam_re * lam_re + lam_im * lam_im
    num_re = abar_re - 1.0
    k_re = (num_re * lam_re + abar_im * lam_im) / den
    k_im = (abar_im * lam_re - num_re * lam_im) / den
    bb_re = k_re[..., None] * b_re - k_im[..., None] * b_im
    bb_im = k_re[..., None] * b_im + k_im[..., None] * b_re
    eye = jnp.eye(S5_GROUPS, dtype=F32)

    def in_blocks(bb):
        t = jnp.transpose(bb, (0, 2, 1))[:, :, None, :] * eye[:, None, :, None]
        return t.reshape(WIDTH, S5_LANES)

    def out_blocks(cc):
        t = jnp.transpose(cc, (0, 2, 1))[:, :, None, :] * eye[:, None, :, None]
        return t.reshape(S5_LANES, WIDTH)

    wb = jnp.concatenate([in_blocks(bb_re), in_blocks(bb_im)], axis=1).astype(BF16)
    wc = jnp.concatenate([out_blocks(c_re), -out_blocks(c_im)], axis=0).astype(BF16)
    rep = lambda a: jnp.broadcast_to(a.reshape(1, S5_LANES), (bsz, S5_LANES))
    return wb, wc, rep(abar_re), rep(abar_im)


def _merge_kernel(ya_ref, yb_ref, yc_ref, gate_ref, x_ref, wbr_ref, bg_ref, wo_ref, g_ref, b_ref,
                  o_ref, *, alpha):
    tm = x_ref.shape[0]
    half = tm // 2
    spans = ((0, half), (half, tm))
    prs = [[jnp.dot(y_ref[a:b, :], wbr_ref[r], preferred_element_type=F32)
            for r, y_ref in enumerate((ya_ref, yb_ref, yc_ref))] for a, b in spans]
    for (a, b), pr in zip(spans, prs):
        merged = None
        for r in range(N_BRANCH):
            gt = _sigmoid(gate_ref[a:b, r * D_MODEL:(r + 1) * D_MODEL] + bg_ref[r:r + 1, :])
            merged = gt * pr[r] if merged is None else merged + gt * pr[r]
        mix = jnp.dot(merged.astype(BF16), wo_ref[...], preferred_element_type=F32)
        o_ref[a:b, :] = _layer_norm(alpha * x_ref[a:b, :] + mix, g_ref[...], b_ref[...])


def _merge(ya, yb, yc, gates, x2, wbr, bg, wo, g, b, alpha, tm=512):
    m = x2.shape[0]
    row = lambda i: (i, 0)
    kernel = functools.partial(_merge_kernel, alpha=alpha)
    return pl.pallas_call(
        kernel,
        grid=(m // tm,),
        in_specs=[
            pl.BlockSpec((tm, WIDTH), row),
            pl.BlockSpec((tm, WIDTH), row),
            pl.BlockSpec((tm, WIDTH), row),
            pl.BlockSpec((tm, N_BRANCH * D_MODEL), row),
            pl.BlockSpec((tm, D_MODEL), row),
            _full(wbr.shape), _full(bg.shape), _full(wo.shape), _full(g.shape), _full(b.shape),
        ],
        out_specs=pl.BlockSpec((tm, D_MODEL), row),
        out_shape=jax.ShapeDtypeStruct((m, D_MODEL), F32),
        compiler_params=_params(("parallel",)),
        name="merge_ln1",
    )(ya, yb, yc, gates, x2, wbr, bg, wo, g, b)


def _ffn_kernel(x_ref, halo_ref, wu_ref, cw_ref, cb_ref, wd_ref, g_ref, b_ref, o_ref, *,
                alpha, nt, tf, rb):
    tm = x_ref.shape[0]
    first = (pl.program_id(0) % nt) == 0
    x = x_ref[...]
    halo = jnp.where(first, 0.0, halo_ref[...])
    xf = jnp.concatenate([halo, x], axis=0)
    nrb = tm // rb
    xs = [xf[(0 if r == 0 else SUBLANE + r * rb):SUBLANE + (r + 1) * rb].astype(BF16)
          for r in range(nrb)]

    def up(j, r):
        cv, cg = j * tf, D_FF + j * tf
        return (jnp.dot(xs[r], wu_ref[:, cv:cv + tf], preferred_element_type=F32),
                jnp.dot(xs[r], wu_ref[:, cg:cg + tf], preferred_element_type=F32))

    def conv(h, c0, r):
        acc = cb_ref[:, c0:c0 + tf]
        for k in range(FFN_CONV):
            off = SUBLANE - (FFN_CONV - 1) + k + r * rb
            acc = acc + cw_ref[k:k + 1, c0:c0 + tf] * h[off:off + rb, :]
        return acc

    nch = D_FF // tf
    outs = [None] * nrb
    pieces = [up(0, r) for r in range(nrb)]
    for j in range(nch):
        cv, cg = j * tf, D_FF + j * tf
        hv = jnp.concatenate([p[0] for p in pieces], axis=0)
        hg = jnp.concatenate([p[1] for p in pieces], axis=0)
        nxt = []
        for r in range(nrb):
            if j + 1 < nch:
                nxt.append(up(j + 1, r))
            val = conv(hv, cv, r)
            gate = conv(hg, cg, r)
            act = (gate * _sigmoid(gate) * val).astype(BF16)
            part = jnp.dot(act, wd_ref[cv:cv + tf, :], preferred_element_type=F32)
            outs[r] = part if outs[r] is None else outs[r] + part
        pieces = nxt
    out = jnp.concatenate(outs, axis=0)
    o_ref[...] = _layer_norm(alpha * x + out, g_ref[...], b_ref[...])


def _ffn(x2, wu, cw, cb, wd, g, b, seq, alpha, tm=1024, tf=256, rb=256):
    m = x2.shape[0]
    nt = seq // tm
    row = lambda i: (i, 0)
    kernel = functools.partial(_ffn_kernel, alpha=alpha, nt=nt, tf=tf, rb=rb)
    return pl.pallas_call(
        kernel,
        grid=(m // tm,),
        in_specs=[
            pl.BlockSpec((tm, D_MODEL), row),
            pl.BlockSpec((SUBLANE, D_MODEL), lambda i: (jnp.maximum(i * (tm // SUBLANE) - 1, 0), 0)),
            _full(wu.shape), _full(cw.shape), _full(cb.shape), _full(wd.shape),
            _full(g.shape), _full(b.shape),
        ],
        out_specs=pl.BlockSpec((tm, D_MODEL), row),
        out_shape=jax.ShapeDtypeStruct((m, D_MODEL), F32),
        compiler_params=_params(("parallel",)),
        name="ffn_ln2",
    )(x2, x2, wu, cw, cb, wd, g, b)


def _in_proj_weight(w_in):
    o = 0
    parts = {}
    for name, size in (("q", WIDTH), ("k", WIDTH), ("v", WIDTH), ("f", N_HEADS), ("z", WIDTH),
                       ("xbc", SSD_CONV_DIM), ("dt", N_HEADS), ("u", WIDTH),
                       ("gate", N_BRANCH * D_MODEL)):
        parts[name] = w_in[:, o:o + size]
        o += size
    small = jnp.pad(jnp.concatenate([parts["f"], parts["dt"]], axis=1),
                    ((0, 0), (0, LANE - 2 * N_HEADS)))
    q_scale = HEAD_DIM ** -0.5 * LOG2E
    w = jnp.concatenate([parts["q"] * q_scale, parts["k"], parts["xbc"], parts["z"],
                         parts["u"], parts["gate"], small], axis=1)
    return w.astype(BF16), parts["v"].T.astype(BF16)


def kernel(x, w_in, fox_f_bias, ssd_conv_w, ssd_conv_b, ssd_dt_bias, ssd_a_log, ssd_d, ssd_norm_w, s5_a_re, s5_a_im, s5_b_re, s5_b_im, s5_c_re, s5_c_im, s5_log_step, s5_d, s5_w_glu, s5_b_glu, w_branch, b_gate, w_out, ln1_g, ln1_b, ffn_w_up, ffn_conv_w, ffn_conv_b, ffn_w_down, ln2_g, ln2_b):
    bsz, seq, _ = x.shape
    depth = w_in.shape[0]
    alpha = (2 * depth) ** 0.25
    tk = 256
    pq, pk = _bias_placement()
    erep = _head_repeat_matrix()
    rep = lambda a: jnp.repeat(a, HEAD_DIM).reshape(1, WIDTH)
    row = lambda a: a.reshape(1, -1)
    x2 = x.reshape(bsz * seq, D_MODEL)
    for i in range(depth):
        w, wvt = _in_proj_weight(w_in[i])
        q, k, vt, xbc, z, u, gates, fl = _in_proj(x2, w, wvt, tk)
        f_bias = jnp.pad(fox_f_bias[i], (0, LANE - N_HEADS)).reshape(1, LANE)
        c3 = _fox_cum(fl, f_bias, bsz, seq)
        ya = _fox_attn(q, k, vt, c3, pq, pk, bsz, seq, tk)
        yb = _ssd(xbc, z, fl, erep, ssd_conv_w[i], row(ssd_conv_b[i]), rep(ssd_dt_bias[i]),
                  rep(ssd_a_log[i]), rep(ssd_d[i]), row(ssd_norm_w[i]), bsz, seq)
        wb, wc, a_re, a_im = _s5_weights(s5_a_re[i], s5_a_im[i], s5_b_re[i], s5_b_im[i],
                                         s5_c_re[i], s5_c_im[i], s5_log_step[i], bsz)
        yc = _s5(u.reshape(bsz, seq, WIDTH), wb, wc, a_re, a_im, row(s5_d[i]),
                 s5_w_glu[i].astype(BF16), row(s5_b_glu[i]), bsz, seq)
        x2 = _merge(ya, yb, yc.reshape(bsz * seq, WIDTH), gates, x2, w_branch[i].astype(BF16),
                    b_gate[i], w_out[i].astype(BF16), row(ln1_g[i]), row(ln1_b[i]), alpha)
        x2 = _ffn(x2, ffn_w_up[i].astype(BF16), ffn_conv_w[i], row(ffn_conv_b[i]),
                  ffn_w_down[i].astype(BF16), row(ln2_g[i]), row(ln2_b[i]), seq, alpha)
    return x2.reshape(bsz, seq, D_MODEL)
```

```python
import functools
import math

import jax
import jax.numpy as jnp
from jax import lax
from jax.experimental import pallas as pl
from jax.experimental.pallas import tpu as pltpu

F32 = jnp.float32
BF16 = jnp.bfloat16

D_MODEL = 1024
N_HEADS = 8
HEAD_DIM = 64
WIDTH = N_HEADS * HEAD_DIM
SSD_STATE = 64
SSD_GROUPS = 2
SSD_CONV = 4
SSD_CHUNK = 128
SSD_CONV_DIM = WIDTH + 2 * SSD_GROUPS * SSD_STATE
S5_GROUP = 16
S5_GROUPS = WIDTH // S5_GROUP
S5_STATE = 64
S5_LANES = S5_GROUPS * S5_STATE
S5_SLAB_GROUPS = 16
D_FF = 2816
FFN_CONV = 3
N_BRANCH = 3
LN_EPS = 1e-5
RMS_EPS = 1e-5
LANE = 128
SUBLANE = 8
NEG_BIG = -1e30
VMEM_LIMIT = 56 * 1024 * 1024

QK_WIDE = N_HEADS * LANE
AUG0 = HEAD_DIM
C3_ONE = 3 * N_HEADS
LOG2E = math.log2(math.e)
ONES_ROWS = 16

_C_Q = 0
_C_K = _C_Q + WIDTH
_C_XBC = _C_K + WIDTH
_C_Z = _C_XBC + SSD_CONV_DIM
_C_U = _C_Z + WIDTH
_C_GATE = _C_U + WIDTH
_C_FL = _C_GATE + N_BRANCH * D_MODEL
_C_END = _C_FL + LANE


def _params(sem):
    return pltpu.CompilerParams(dimension_semantics=sem, vmem_limit_bytes=VMEM_LIMIT)


def _full(shape):
    n = len(shape)
    return pl.BlockSpec(shape, lambda *_: (0,) * n, pipeline_mode=pl.Buffered(1))


def _sigmoid(x):
    return 1.0 / (1.0 + jnp.exp(-x))


def _softplus(x):
    return jnp.maximum(x, 0.0) + jnp.log1p(jnp.exp(-jnp.abs(x)))


def _cumsum_rows(x):
    n = x.shape[0]
    row = lax.broadcasted_iota(jnp.int32, x.shape, 0)
    s = 1
    while s < n:
        x = x + jnp.where(row >= s, pltpu.roll(x, s, 0), 0.0)
        s *= 2
    return x


def _layer_norm(v, g, b):
    mu = jnp.mean(v, axis=-1, keepdims=True)
    d = v - mu
    var = jnp.mean(d * d, axis=-1, keepdims=True)
    return d * lax.rsqrt(var + LN_EPS) * g + b


def _in_proj_kernel(x_ref, w_ref, wvt_ref, q_ref, k_ref, vt_ref, xbc_ref, z_ref, u_ref,
                    gate_ref, fl_ref):
    xb = x_ref[...].astype(BF16)
    tk = vt_ref.shape[2]

    def mm(c0, c1):
        return jnp.dot(xb, w_ref[:, c0:c1], preferred_element_type=F32)

    lo = lax.broadcasted_iota(jnp.int32, (1, LANE), 1) < HEAD_DIM

    def widen(y):
        blocks = []
        for j in range(N_HEADS // 2):
            blk = y[:, j * LANE:(j + 1) * LANE]
            blocks.append(jnp.where(lo, blk, 0.0))
            blocks.append(jnp.where(lo, pltpu.roll(blk, HEAD_DIM, 1), 0.0))
        return jnp.concatenate(blocks, axis=1)

    q_ref[...] = widen(mm(_C_Q, _C_K)).astype(BF16)
    k_ref[...] = widen(mm(_C_K, _C_XBC)).astype(BF16)
    for r in range(vt_ref.shape[0]):
        vt_ref[r] = lax.dot_general(wvt_ref[...], xb[r * tk:(r + 1) * tk], (((1,), (1,)), ((), ())),
                                    preferred_element_type=F32).astype(BF16)
    xbc_ref[...] = mm(_C_XBC, _C_Z)
    z_ref[...] = mm(_C_Z, _C_U)
    u_ref[...] = mm(_C_U, _C_GATE)
    gate_ref[...] = mm(_C_GATE, _C_FL).astype(BF16)
    fl_ref[...] = mm(_C_FL, _C_END)


def _in_proj(x2, w, wvt, tk, tm=512):
    m = x2.shape[0]
    row = lambda i: (i, 0)
    out_shape = (
        jax.ShapeDtypeStruct((m, QK_WIDE), BF16),
        jax.ShapeDtypeStruct((m, QK_WIDE), BF16),
        jax.ShapeDtypeStruct((m // tk, WIDTH, tk), BF16),
        jax.ShapeDtypeStruct((m, SSD_CONV_DIM), F32),
        jax.ShapeDtypeStruct((m, WIDTH), F32),
        jax.ShapeDtypeStruct((m, WIDTH), F32),
        jax.ShapeDtypeStruct((m, N_BRANCH * D_MODEL), BF16),
        jax.ShapeDtypeStruct((m, LANE), F32),
    )
    out_specs = (
        pl.BlockSpec((tm, QK_WIDE), row),
        pl.BlockSpec((tm, QK_WIDE), row),
        pl.BlockSpec((tm // tk, WIDTH, tk), lambda i: (i, 0, 0)),
        pl.BlockSpec((tm, SSD_CONV_DIM), row),
        pl.BlockSpec((tm, WIDTH), row),
        pl.BlockSpec((tm, WIDTH), row),
        pl.BlockSpec((tm, N_BRANCH * D_MODEL), row),
        pl.BlockSpec((tm, LANE), row),
    )
    return pl.pallas_call(
        _in_proj_kernel,
        grid=(m // tm,),
        in_specs=[pl.BlockSpec((tm, D_MODEL), row), _full(w.shape), _full(wvt.shape)],
        out_specs=out_specs,
        out_shape=out_shape,
        compiler_params=_params(("parallel",)),
        name="in_proj",
    )(x2, w, wvt)


def _fox_cum_kernel(fl_ref, bias_ref, c3_ref):
    v = fl_ref[...] + bias_ref[...]
    ls = jnp.minimum(v, 0.0) - jnp.log1p(jnp.exp(-jnp.abs(v)))
    c = _cumsum_rows(ls) * LOG2E
    hi = c.astype(BF16).astype(F32)
    r1 = c - hi
    mid = r1.astype(BF16).astype(F32)
    lo = (r1 - mid).astype(BF16).astype(F32)
    lane = lax.broadcasted_iota(jnp.int32, c.shape, 1)
    out = jnp.where(lane < N_HEADS, hi,
                    jnp.where(lane < 2 * N_HEADS, pltpu.roll(mid, N_HEADS, 1),
                              jnp.where(lane < C3_ONE, pltpu.roll(lo, 2 * N_HEADS, 1),
                                        jnp.where(lane == C3_ONE, 1.0, 0.0))))
    c3_ref[...] = out.astype(BF16)


def _fox_cum(fl, bias_pad, bsz, seq):
    return pl.pallas_call(
        _fox_cum_kernel,
        grid=(bsz,),
        in_specs=[pl.BlockSpec((seq, LANE), lambda b: (b, 0)), _full((1, LANE))],
        out_specs=pl.BlockSpec((seq, LANE), lambda b: (b, 0)),
        out_shape=jax.ShapeDtypeStruct(fl.shape, BF16),
        compiler_params=_params(("parallel",)),
        name="fox_cum",
    )(fl, bias_pad)


def _bias_placement():
    import numpy as np
    pq = np.zeros((LANE, QK_WIDE), np.float32)
    pk = np.zeros((LANE, QK_WIDE), np.float32)
    for h in range(N_HEADS):
        base = h * LANE + AUG0
        for piece in range(3):
            pq[piece * N_HEADS + h, base + piece] = 1.0
            pq[C3_ONE, base + 3 + piece] = 1.0
            pk[C3_ONE, base + piece] = 1.0
            pk[piece * N_HEADS + h, base + 3 + piece] = -1.0
    return jnp.asarray(pq, BF16), jnp.asarray(pk, BF16)


def _fox_attn_kernel(q_ref, k_ref, vt_ref, c3q_ref, c3k_ref, pq_ref, pk_ref, o_ref, ka_ref, sa_ref,
                     sb_ref, *, tq, tk, rows_k):
    qi = pl.program_id(1)
    seq = k_ref.shape[0]

    @pl.when(qi == 0)
    def _():
        def fill(i, _):
            r0 = pl.multiple_of(i * rows_k, rows_k)
            aug = jnp.dot(c3k_ref[pl.ds(r0, rows_k), :], pk_ref[...], preferred_element_type=F32)
            ka_ref[pl.ds(r0, rows_k), :] = (k_ref[pl.ds(r0, rows_k), :].astype(F32)
                                            + aug).astype(BF16)
            return 0
        lax.fori_loop(0, seq // rows_k, fill, 0)

    qa = (q_ref[...].astype(F32)
          + jnp.dot(c3q_ref[...], pq_ref[...], preferred_element_type=F32)).astype(BF16)
    krow = lax.broadcasted_iota(jnp.int32, (tk, tq), 0)
    qcol = lax.broadcasted_iota(jnp.int32, (tk, tq), 1)
    nkb = tq // tk
    nt = (((1,), (1,)), ((), ()))
    ones_rows = jnp.ones((ONES_ROWS, tk), BF16)

    def tree(fn, s):
        parts = [s[i * SUBLANE:(i + 1) * SUBLANE] for i in range(s.shape[0] // SUBLANE)]
        while len(parts) > 1:
            parts = [fn(parts[i], parts[i + 1]) for i in range(0, len(parts), 2)]
        return parts[0]

    hpb = sa_ref.shape[0]
    for j in range(N_HEADS // hpb):
        heads = tuple(range(j * hpb, (j + 1) * hpb))
        qh = tuple(qa[:, h * LANE:(h + 1) * LANE] for h in heads)

        def scores(buf, kb, heads=heads, qh=qh):
            r0 = pl.multiple_of(kb * tk, tk)
            for e, h in enumerate(heads):
                buf[e] = lax.dot_general(ka_ref[pl.ds(r0, tk), h * LANE:(h + 1) * LANE], qh[e], nt,
                                         preferred_element_type=F32)

        def absorb(buf, kb, stats, diag, nxt, heads=heads):
            if nxt is not None:
                scores(*nxt)
            new = []
            for e, h in enumerate(heads):
                m, acc = stats[e]
                se = buf[e]
                if diag is not None:
                    se = jnp.where(krow + diag * tk <= qcol, se, NEG_BIG)
                m_new = jnp.maximum(m, jnp.max(tree(jnp.maximum, se), axis=0, keepdims=True))
                p = jnp.exp2(se - m_new)
                alpha = jnp.exp2(m - m_new)
                vth = jnp.concatenate([vt_ref[kb, h * HEAD_DIM:(h + 1) * HEAD_DIM, :], ones_rows],
                                      axis=0)
                pv = jnp.dot(vth, p.astype(BF16), preferred_element_type=F32)
                new.append((m_new, alpha * acc + pv))
            return tuple(new)

        def step(t, stats):
            kb = nkb * t
            stats = absorb(sa_ref, kb, stats, None, (sb_ref, kb + 1))
            return absorb(sb_ref, kb + 1, stats, None, (sa_ref, kb + 2))

        init1 = (jnp.full((1, tq), NEG_BIG, F32), jnp.zeros((HEAD_DIM + ONES_ROWS, tq), F32))
        scores(sa_ref, 0)
        stats = lax.fori_loop(0, qi, step, (init1,) * hpb)
        kb = nkb * qi
        stats = absorb(sa_ref, kb, stats, 0, (sb_ref, kb + 1))
        stats = absorb(sb_ref, kb + 1, stats, 1, None)
        for e in range(0, hpb, 2):
            outs = []
            for _, acc in (stats[e], stats[e + 1]):
                outs.append(acc[:HEAD_DIM] / acc[HEAD_DIM:HEAD_DIM + 1])
            out_t = jnp.concatenate(outs, axis=0)
            c0 = (heads[e] // 2) * LANE
            o_ref[:, c0:c0 + LANE] = out_t.T.astype(o_ref.dtype)


def _fox_attn(q, k, vt, c3, pq, pk, bsz, seq, tk, hpb=4):
    m = q.shape[0]
    tq = 2 * tk
    nq = seq // tq
    vt4 = vt.reshape(bsz, seq // tk, WIDTH, tk)
    kernel = functools.partial(_fox_attn_kernel, tq=tq, tk=tk, rows_k=min(512, seq))
    return pl.pallas_call(
        kernel,
        grid=(bsz, nq),
        in_specs=[
            pl.BlockSpec((tq, QK_WIDE), lambda b, i: (b * nq + i, 0)),
            pl.BlockSpec((seq, QK_WIDE), lambda b, i: (b, 0)),
            pl.BlockSpec((None, seq // tk, WIDTH, tk), lambda b, i: (b, 0, 0, 0)),
            pl.BlockSpec((tq, LANE), lambda b, i: (b * nq + i, 0)),
            pl.BlockSpec((seq, LANE), lambda b, i: (b, 0)),
            _full(pq.shape), _full(pk.shape),
        ],
        out_specs=pl.BlockSpec((tq, WIDTH), lambda b, i: (b * nq + i, 0)),
        out_shape=jax.ShapeDtypeStruct((m, WIDTH), BF16),
        scratch_shapes=[pltpu.VMEM((seq, QK_WIDE), BF16), pltpu.VMEM((hpb, tk, tq), F32),
                        pltpu.VMEM((hpb, tk, tq), F32)],
        compiler_params=_params(("parallel", "arbitrary")),
        name="fox_attn",
    )(q, k, vt4, c3, c3, pq, pk)


def _ssd_kernel(xbc_ref, halo_ref, z_ref, fl_ref, erep_ref, cw_ref, cb_ref, dtb_ref, alog_ref, d_ref,
                nw_ref, o_ref, state_ref):
    c = pl.program_id(1)
    q = SSD_CHUNK

    @pl.when(c == 0)
    def _():
        state_ref[...] = jnp.zeros_like(state_ref)

    halo = jnp.where(c > 0, halo_ref[...], 0.0)
    full = jnp.concatenate([halo, xbc_ref[...]], axis=0)
    cw = cw_ref[...]
    acc = cb_ref[...]
    for k in range(SSD_CONV):
        off = SUBLANE - (SSD_CONV - 1) + k
        acc = acc + cw[k:k + 1, :] * full[off:off + q, :]
    xc = acc * _sigmoid(acc)
    xs = xc[:, :WIDTH]
    bm = xc[:, WIDTH:WIDTH + LANE]
    cm = xc[:, WIDTH + LANE:]

    dt8 = _softplus(fl_ref[...] + dtb_ref[...])
    hi = dt8.astype(BF16)
    r1 = dt8 - hi.astype(F32)
    mid = r1.astype(BF16)
    low = (r1 - mid.astype(F32)).astype(BF16)
    dt = jnp.dot(jnp.concatenate([hi, mid, low], axis=1), erep_ref[...],
                 preferred_element_type=F32)
    adt = dt * (-jnp.exp(alog_ref[...]))
    a_cs = _cumsum_rows(adt)
    a_tot = a_cs[q - 1:q, :]
    xdt = xs * dt
    xw = (xdt * jnp.exp(a_tot - a_cs)).astype(BF16)
    dec_out = jnp.exp(a_cs)
    dec_tot = jnp.exp(a_tot)
    a_cs_t = a_cs.T
    xdt_b = xdt.astype(BF16)

    lo = lax.broadcasted_iota(jnp.int32, (1, LANE), 1) < SSD_STATE
    tril = (lax.broadcasted_iota(jnp.int32, (q, q), 0)
            >= lax.broadcasted_iota(jnp.int32, (q, q), 1))
    cm_g = (jnp.where(lo, cm, 0.0).astype(BF16), jnp.where(lo, 0.0, cm).astype(BF16))
    bm_g = (jnp.where(lo, bm, 0.0).astype(BF16), jnp.where(lo, 0.0, bm).astype(BF16))
    bm_b = bm.astype(BF16)
    nt = (((1,), (1,)), ((), ()))
    tn = (((0,), (0,)), ((), ()))
    cb = tuple(lax.dot_general(cm_g[g], bm_b, nt, preferred_element_type=F32) for g in (0, 1))

    ys = []
    for j in range(N_HEADS // 2):
        g = j // (N_HEADS // 2 // SSD_GROUPS)
        c0, c1 = j * LANE, (j + 1) * LANE
        yd = []
        for e in (0, 1):
            hl = (2 * j + e) * HEAD_DIM
            diff = a_cs[:, hl:hl + 1] - a_cs_t[hl:hl + 1, :]
            lm = jnp.exp(jnp.where(tril, diff, NEG_BIG))
            yd.append(jnp.dot((cb[g] * lm).astype(BF16), xdt_b[:, c0:c1],
                              preferred_element_type=F32))
        y_diag = jnp.where(lo, yd[0], yd[1])
        st = state_ref[j]
        y_off = jnp.dot(cm_g[g], st.astype(BF16), preferred_element_type=F32) * dec_out[:, c0:c1]
        state_ref[j] = dec_tot[:, c0:c1] * st + lax.dot_general(
            bm_g[g], xw[:, c0:c1], tn, preferred_element_type=F32)
        ys.append(y_diag + y_off)
    y = jnp.concatenate(ys, axis=1) + d_ref[...] * xs
    zf = z_ref[...]
    y = y * (zf * _sigmoid(zf))
    gw = WIDTH // SSD_GROUPS
    outs = []
    for g in range(SSD_GROUPS):
        yg = y[:, g * gw:(g + 1) * gw]
        outs.append(yg * lax.rsqrt(jnp.mean(yg * yg, axis=-1, keepdims=True) + RMS_EPS))
    o_ref[...] = (jnp.concatenate(outs, axis=1) * nw_ref[...]).astype(o_ref.dtype)


def _head_repeat_matrix():
    import numpy as np
    e = np.zeros((LANE, WIDTH), np.float32)
    for h in range(N_HEADS):
        e[N_HEADS + h, h * HEAD_DIM:(h + 1) * HEAD_DIM] = 1.0
    return jnp.asarray(np.concatenate([e, e, e], axis=0), BF16)


def _ssd(xbc, z, fl, erep, cw, cb, dtb_rep, alog_rep, d_rep, nw, bsz, seq):
    m = xbc.shape[0]
    q = SSD_CHUNK
    nc = seq // q
    row = lambda b, c: (b * nc + c, 0)
    halo = lambda b, c: (jnp.maximum((b * nc + c) * (q // SUBLANE) - 1, 0), 0)
    return pl.pallas_call(
        _ssd_kernel,
        grid=(bsz, nc),
        in_specs=[
            pl.BlockSpec((q, SSD_CONV_DIM), row),
            pl.BlockSpec((SUBLANE, SSD_CONV_DIM), halo),
            pl.BlockSpec((q, WIDTH), row),
            pl.BlockSpec((q, LANE), row),
            _full(erep.shape),
            _full(cw.shape), _full(cb.shape), _full(dtb_rep.shape), _full(alog_rep.shape),
            _full(d_rep.shape), _full(nw.shape),
        ],
        out_specs=pl.BlockSpec((q, WIDTH), row),
        out_shape=jax.ShapeDtypeStruct((m, WIDTH), BF16),
        scratch_shapes=[pltpu.VMEM((N_HEADS // 2, LANE, LANE), F32)],
        compiler_params=_params(("parallel", "arbitrary")),
        name="ssd",
    )(xbc, xbc, z, fl, erep, cw, cb, dtb_rep, alog_rep, d_rep, nw)


def _s5_kernel(u_ref, wb_ref, wc_ref, are_ref, aim_ref, d_ref, wg_ref, bg_ref, o_ref,
               bu_ref, h_ref, *, steps, bsz):
    n = S5_LANES

    @pl.when(pl.program_id(0) == 0)
    def _():
        h_ref[...] = jnp.zeros_like(h_ref)

    u = pltpu.einshape("bth->tbh", u_ref[...]).reshape(steps * bsz, WIDTH)
    ub = u.astype(BF16)
    nslab = S5_GROUPS // S5_SLAB_GROUPS
    sw, sn = S5_SLAB_GROUPS * S5_GROUP, S5_SLAB_GROUPS * S5_STATE
    for c in range(nslab):
        for part in range(2):
            c0 = part * n + c * sn
            bu_ref[:, c0:c0 + sn] = jnp.dot(ub[:, c * sw:(c + 1) * sw],
                                            wb_ref[c * sw:(c + 1) * sw, c0:c0 + sn],
                                            preferred_element_type=F32)
    ar = are_ref[...]
    ai = aim_ref[...]

    def step(t, carry):
        hr, hi = carry
        r0 = pl.multiple_of(t * bsz, bsz)
        br = bu_ref[pl.ds(r0, bsz), 0:n]
        bi = bu_ref[pl.ds(r0, bsz), n:2 * n]
        nr = ar * hr - ai * hi + br
        ni = ar * hi + ai * hr + bi
        bu_ref[pl.ds(r0, bsz), 0:n] = nr
        bu_ref[pl.ds(r0, bsz), n:2 * n] = ni
        return nr, ni

    hr, hi = lax.fori_loop(0, steps, step, (h_ref[:, 0:n], h_ref[:, n:2 * n]))
    h_ref[:, 0:n] = hr
    h_ref[:, n:2 * n] = hi
    ys = []
    for c in range(nslab):
        yc = None
        for part in range(2):
            c0 = part * n + c * sn
            t = jnp.dot(bu_ref[:, c0:c0 + sn].astype(BF16), wc_ref[c0:c0 + sn, c * sw:(c + 1) * sw],
                        preferred_element_type=F32)
            yc = t if yc is None else yc + t
        ys.append(yc)
    y = jnp.concatenate(ys, axis=1) + d_ref[...] * u
    gy = jax.nn.gelu(y)
    gl = jnp.dot(gy.astype(BF16), wg_ref[...], preferred_element_type=F32) + bg_ref[...]
    out = (gy * _sigmoid(gl)).reshape(steps, bsz, WIDTH)
    o_ref[...] = pltpu.einshape("tbh->bth", out).astype(o_ref.dtype)


def _s5(u3, wb, wc, a_re, a_im, d, wg, bg, bsz, seq, steps=64):
    rows = steps * bsz
    kernel = functools.partial(_s5_kernel, steps=steps, bsz=bsz)
    return pl.pallas_call(
        kernel,
        grid=(seq // steps,),
        in_specs=[
            pl.BlockSpec((bsz, steps, WIDTH), lambda i: (0, i, 0)),
            _full(wb.shape), _full(wc.shape), _full(a_re.shape), _full(a_im.shape),
            _full(d.shape), _full(wg.shape), _full(bg.shape),
        ],
        out_specs=pl.BlockSpec((bsz, steps, WIDTH), lambda i: (0, i, 0)),
        out_shape=jax.ShapeDtypeStruct(u3.shape, BF16),
        scratch_shapes=[pltpu.VMEM((rows, 2 * S5_LANES), F32),
                        pltpu.VMEM((bsz, 2 * S5_LANES), F32)],
        compiler_params=_params(("arbitrary",)),
        name="s5",
    )(u3, wb, wc, a_re, a_im, d, wg, bg)


def _s5_weights(a_re, a_im, b_re, b_im, c_re, c_im, log_step, bsz):
    lam_re = jnp.minimum(a_re, -1e-4)
    lam_im = a_im
    step = jnp.exp(log_step)[:, None]
    mag = jnp.exp(lam_re * step)
    abar_re = mag * jnp.cos(lam_im * step)
    abar_im = mag * jnp.sin(lam_im * step)
    den = lam_re * lam_re + lam_im * lam_im
    num_re = abar_re - 1.0
    k_re = (num_re * lam_re + abar_im * lam_im) / den
    k_im = (abar_im * lam_re - num_re * lam_im) / den
    bb_re = k_re[..., None] * b_re - k_im[..., None] * b_im
    bb_im = k_re[..., None] * b_im + k_im[..., None] * b_re
    eye = jnp.eye(S5_GROUPS, dtype=F32)

    def in_blocks(bb):
        t = jnp.transpose(bb, (0, 2, 1))[:, :, None, :] * eye[:, None, :, None]
        return t.reshape(WIDTH, S5_LANES)

    def out_blocks(cc):
        t = jnp.transpose(cc, (0, 2, 1))[:, :, None, :] * eye[:, None, :, None]
        return t.reshape(S5_LANES, WIDTH)

    wb = jnp.concatenate([in_blocks(bb_re), in_blocks(bb_im)], axis=1).astype(BF16)
    wc = jnp.concatenate([out_blocks(c_re), -out_blocks(c_im)], axis=0).astype(BF16)
    rep = lambda a: jnp.broadcast_to(a.reshape(1, S5_LANES), (bsz, S5_LANES))
    return wb, wc, rep(abar_re), rep(abar_im)


def _merge_kernel(ya_ref, yb_ref, yc_ref, gate_ref, x_ref, wbr_ref, bg_ref, wo_ref, g_ref, b_ref,
                  o_ref, *, alpha):
    tm = x_ref.shape[0]
    half = tm // 2
    spans = ((0, half), (half, tm))
    prs = [[jnp.dot(y_ref[a:b, :], wbr_ref[r], preferred_element_type=F32)
            for r, y_ref in enumerate((ya_ref, yb_ref, yc_ref))] for a, b in spans]
    for (a, b), pr in zip(spans, prs):
        merged = None
        for r in range(N_BRANCH):
            gt = _sigmoid(gate_ref[a:b, r * D_MODEL:(r + 1) * D_MODEL] + bg_ref[r:r + 1, :])
            merged = gt * pr[r] if merged is None else merged + gt * pr[r]
        mix = jnp.dot(merged.astype(BF16), wo_ref[...], preferred_element_type=F32)
        o_ref[a:b, :] = _layer_norm(alpha * x_ref[a:b, :] + mix, g_ref[...], b_ref[...])


def _merge(ya, yb, yc, gates, x2, wbr, bg, wo, g, b, alpha, tm=512):
    m = x2.shape[0]
    row = lambda i: (i, 0)
    kernel = functools.partial(_merge_kernel, alpha=alpha)
    return pl.pallas_call(
        kernel,
        grid=(m // tm,),
        in_specs=[
            pl.BlockSpec((tm, WIDTH), row),
            pl.BlockSpec((tm, WIDTH), row),
            pl.BlockSpec((tm, WIDTH), row),
            pl.BlockSpec((tm, N_BRANCH * D_MODEL), row),
            pl.BlockSpec((tm, D_MODEL), row),
            _full(wbr.shape), _full(bg.shape), _full(wo.shape), _full(g.shape), _full(b.shape),
        ],
        out_specs=pl.BlockSpec((tm, D_MODEL), row),
        out_shape=jax.ShapeDtypeStruct((m, D_MODEL), F32),
        compiler_params=_params(("parallel",)),
        name="merge_ln1",
    )(ya, yb, yc, gates, x2, wbr, bg, wo, g, b)


def _ffn_kernel(x_ref, halo_ref, wu_ref, cw_ref, cb_ref, wd_ref, g_ref, b_ref, o_ref, *,
                alpha, nt, tf, rb):
    tm = x_ref.shape[0]
    first = (pl.program_id(0) % nt) == 0
    x = x_ref[...]
    halo = jnp.where(first, 0.0, halo_ref[...])
    xf = jnp.concatenate([halo, x], axis=0)
    nrb = tm // rb
    xs = [xf[(0 if r == 0 else SUBLANE + r * rb):SUBLANE + (r + 1) * rb].astype(BF16)
          for r in range(nrb)]

    def up(j, r):
        cv, cg = j * tf, D_FF + j * tf
        return (jnp.dot(xs[r], wu_ref[:, cv:cv + tf], preferred_element_type=F32),
                jnp.dot(xs[r], wu_ref[:, cg:cg + tf], preferred_element_type=F32))

    def conv(h, c0, r):
        acc = cb_ref[:, c0:c0 + tf]
        for k in range(FFN_CONV):
            off = SUBLANE - (FFN_CONV - 1) + k + r * rb
            acc = acc + cw_ref[k:k + 1, c0:c0 + tf] * h[off:off + rb, :]
        return acc

    nch = D_FF // tf
    outs = [None] * nrb
    pieces = [up(0, r) for r in range(nrb)]
    for j in range(nch):
        cv, cg = j * tf, D_FF + j * tf
        hv = jnp.concatenate([p[0] for p in pieces], axis=0)
        hg = jnp.concatenate([p[1] for p in pieces], axis=0)
        nxt = []
        for r in range(nrb):
            if j + 1 < nch:
                nxt.append(up(j + 1, r))
            val = conv(hv, cv, r)
            gate = conv(hg, cg, r)
            act = (gate * _sigmoid(gate) * val).astype(BF16)
            part = jnp.dot(act, wd_ref[cv:cv + tf, :], preferred_element_type=F32)
            outs[r] = part if outs[r] is None else outs[r] + part
        pieces = nxt
    out = jnp.concatenate(outs, axis=0)
    o_ref[...] = _layer_norm(alpha * x + out, g_ref[...], b_ref[...])


def _ffn(x2, wu, cw, cb, wd, g, b, seq, alpha, tm=1024, tf=256, rb=256):
    m = x2.shape[0]
    nt = seq // tm
    row = lambda i: (i, 0)
    kernel = functools.partial(_ffn_kernel, alpha=alpha, nt=nt, tf=tf, rb=rb)
    return pl.pallas_call(
        kernel,
        grid=(m // tm,),
        in_specs=[
            pl.BlockSpec((tm, D_MODEL), row),
            pl.BlockSpec((SUBLANE, D_MODEL), lambda i: (jnp.maximum(i * (tm // SUBLANE) - 1, 0), 0)),
            _full(wu.shape), _full(cw.shape), _full(cb.shape), _full(wd.shape),
            _full(g.shape), _full(b.shape),
        ],
        out_specs=pl.BlockSpec((tm, D_MODEL), row),
        out_shape=jax.ShapeDtypeStruct((m, D_MODEL), F32),
        compiler_params=_params(("parallel",)),
        name="ffn_ln2",
    )(x2, x2, wu, cw, cb, wd, g, b)


def _in_proj_weight(w_in):
    o = 0
    parts = {}
    for name, size in (("q", WIDTH), ("k", WIDTH), ("v", WIDTH), ("f", N_HEADS), ("z", WIDTH),
                       ("xbc", SSD_CONV_DIM), ("dt", N_HEADS), ("u", WIDTH),
                       ("gate", N_BRANCH * D_MODEL)):
        parts[name] = w_in[:, o:o + size]
        o += size
    small = jnp.pad(jnp.concatenate([parts["f"], parts["dt"]], axis=1),
                    ((0, 0), (0, LANE - 2 * N_HEADS)))
    q_scale = HEAD_DIM ** -0.5 * LOG2E
    w = jnp.concatenate([parts["q"] * q_scale, parts["k"], parts["xbc"], parts["z"],
                         parts["u"], parts["gate"], small], axis=1)
    return w.astype(BF16), parts["v"].T.astype(BF16)


def kernel(x, w_in, fox_f_bias, ssd_conv_w, ssd_conv_b, ssd_dt_bias, ssd_a_log, ssd_d, ssd_norm_w, s5_a_re, s5_a_im, s5_b_re, s5_b_im, s5_c_re, s5_c_im, s5_log_step, s5_d, s5_w_glu, s5_b_glu, w_branch, b_gate, w_out, ln1_g, ln1_b, ffn_w_up, ffn_conv_w, ffn_conv_b, ffn_w_down, ln2_g, ln2_b):
    bsz, seq, _ = x.shape
    depth = w_in.shape[0]
    alpha = (2 * depth) ** 0.25
    tk = 256
    pq, pk = _bias_placement()
    erep = _head_repeat_matrix()
    rep = lambda a: jnp.repeat(a, HEAD_DIM).reshape(1, WIDTH)
    row = lambda a: a.reshape(1, -1)
    x2 = x.reshape(bsz * seq, D_MODEL)
    for i in range(depth):
        w, wvt = _in_proj_weight(w_in[i])
        q, k, vt, xbc, z, u, gates, fl = _in_proj(x2, w, wvt, tk)
        f_bias = jnp.pad(fox_f_bias[i], (0, LANE - N_HEADS)).reshape(1, LANE)
        c3 = _fox_cum(fl, f_bias, bsz, seq)
        ya = _fox_attn(q, k, vt, c3, pq, pk, bsz, seq, tk)
        dt_bias = jnp.pad(ssd_dt_bias[i], (N_HEADS, LANE - 2 * N_HEADS)).reshape(1, LANE)
        yb = _ssd(xbc, z, fl, erep, ssd_conv_w[i], row(ssd_conv_b[i]), dt_bias,
                  rep(ssd_a_log[i]), rep(ssd_d[i]), row(ssd_norm_w[i]), bsz, seq)
        wb, wc, a_re, a_im = _s5_weights(s5_a_re[i], s5_a_im[i], s5_b_re[i], s5_b_im[i],
                                         s5_c_re[i], s5_c_im[i], s5_log_step[i], bsz)
        yc = _s5(u.reshape(bsz, seq, WIDTH), wb, wc, a_re, a_im, row(s5_d[i]),
                 s5_w_glu[i].astype(BF16), row(s5_b_glu[i]), bsz, seq)
        x2 = _merge(ya, yb, yc.reshape(bsz * seq, WIDTH), gates, x2, w_branch[i].astype(BF16),
                    b_gate[i], w_out[i].astype(BF16), row(ln1_g[i]), row(ln1_b[i]), alpha)
        x2 = _ffn(x2, ffn_w_up[i].astype(BF16), ffn_conv_w[i], row(ffn_conv_b[i]),
                  ffn_w_down[i].astype(BF16), row(ln2_g[i]), row(ln2_b[i]), seq, alpha)
    return x2.reshape(bsz, seq, D_MODEL)
```

```python
import functools
import math

import jax
import jax.numpy as jnp
from jax import lax
from jax.experimental import pallas as pl
from jax.experimental.pallas import tpu as pltpu

F32 = jnp.float32
BF16 = jnp.bfloat16

D_MODEL = 1024
N_HEADS = 8
HEAD_DIM = 64
WIDTH = N_HEADS * HEAD_DIM
SSD_STATE = 64
SSD_GROUPS = 2
SSD_CONV = 4
SSD_CHUNK = 128
SSD_CONV_DIM = WIDTH + 2 * SSD_GROUPS * SSD_STATE
S5_GROUP = 16
S5_GROUPS = WIDTH // S5_GROUP
S5_STATE = 64
S5_LANES = S5_GROUPS * S5_STATE
S5_SLAB_GROUPS = 16
D_FF = 2816
FFN_CONV = 3
N_BRANCH = 3
LN_EPS = 1e-5
RMS_EPS = 1e-5
LANE = 128
SUBLANE = 8
NEG_BIG = -1e30
VMEM_LIMIT = 56 * 1024 * 1024

QK_WIDE = N_HEADS * LANE
AUG0 = HEAD_DIM
C3_ONE = 3 * N_HEADS
LOG2E = math.log2(math.e)
ONES_ROWS = 16

_C_Q = 0
_C_K = _C_Q + WIDTH
_C_XBC = _C_K + WIDTH
_C_Z = _C_XBC + SSD_CONV_DIM
_C_U = _C_Z + WIDTH
_C_GATE = _C_U + WIDTH
_C_FL = _C_GATE + N_BRANCH * D_MODEL
_C_END = _C_FL + LANE


def _params(sem):
    return pltpu.CompilerParams(dimension_semantics=sem, vmem_limit_bytes=VMEM_LIMIT)


def _full(shape):
    n = len(shape)
    return pl.BlockSpec(shape, lambda *_: (0,) * n, pipeline_mode=pl.Buffered(1))


def _sigmoid(x):
    return 1.0 / (1.0 + jnp.exp(-x))


def _softplus(x):
    return jnp.maximum(x, 0.0) + jnp.log1p(jnp.exp(-jnp.abs(x)))


def _cumsum_rows(x):
    n = x.shape[0]
    row = lax.broadcasted_iota(jnp.int32, x.shape, 0)
    s = 1
    while s < n:
        x = x + jnp.where(row >= s, pltpu.roll(x, s, 0), 0.0)
        s *= 2
    return x


def _layer_norm(v, g, b):
    mu = jnp.mean(v, axis=-1, keepdims=True)
    d = v - mu
    var = jnp.mean(d * d, axis=-1, keepdims=True)
    return d * lax.rsqrt(var + LN_EPS) * g + b


def _in_proj_kernel(x_ref, w_ref, wvt_ref, q_ref, k_ref, vt_ref, xbc_ref, z_ref, u_ref,
                    gate_ref, fl_ref):
    xb = x_ref[...].astype(BF16)
    tk = vt_ref.shape[2]

    def mm(c0, c1):
        return jnp.dot(xb, w_ref[:, c0:c1], preferred_element_type=F32)

    lo = lax.broadcasted_iota(jnp.int32, (1, LANE), 1) < HEAD_DIM

    def widen(y):
        blocks = []
        for j in range(N_HEADS // 2):
            blk = y[:, j * LANE:(j + 1) * LANE]
            blocks.append(jnp.where(lo, blk, 0.0))
            blocks.append(jnp.where(lo, pltpu.roll(blk, HEAD_DIM, 1), 0.0))
        return jnp.concatenate(blocks, axis=1)

    q_ref[...] = widen(mm(_C_Q, _C_K)).astype(BF16)
    k_ref[...] = widen(mm(_C_K, _C_XBC)).astype(BF16)
    for r in range(vt_ref.shape[0]):
        vt_ref[r] = lax.dot_general(wvt_ref[...], xb[r * tk:(r + 1) * tk], (((1,), (1,)), ((), ())),
                                    preferred_element_type=F32).astype(BF16)
    xbc_ref[...] = mm(_C_XBC, _C_Z)
    z_ref[...] = mm(_C_Z, _C_U)
    u_ref[...] = mm(_C_U, _C_GATE)
    gate_ref[...] = mm(_C_GATE, _C_FL).astype(BF16)
    fl_ref[...] = mm(_C_FL, _C_END)


def _in_proj(x2, w, wvt, tk, tm=512):
    m = x2.shape[0]
    row = lambda i: (i, 0)
    out_shape = (
        jax.ShapeDtypeStruct((m, QK_WIDE), BF16),
        jax.ShapeDtypeStruct((m, QK_WIDE), BF16),
        jax.ShapeDtypeStruct((m // tk, WIDTH, tk), BF16),
        jax.ShapeDtypeStruct((m, SSD_CONV_DIM), F32),
        jax.ShapeDtypeStruct((m, WIDTH), F32),
        jax.ShapeDtypeStruct((m, WIDTH), F32),
        jax.ShapeDtypeStruct((m, N_BRANCH * D_MODEL), BF16),
        jax.ShapeDtypeStruct((m, LANE), F32),
    )
    out_specs = (
        pl.BlockSpec((tm, QK_WIDE), row),
        pl.BlockSpec((tm, QK_WIDE), row),
        pl.BlockSpec((tm // tk, WIDTH, tk), lambda i: (i, 0, 0)),
        pl.BlockSpec((tm, SSD_CONV_DIM), row),
        pl.BlockSpec((tm, WIDTH), row),
        pl.BlockSpec((tm, WIDTH), row),
        pl.BlockSpec((tm, N_BRANCH * D_MODEL), row),
        pl.BlockSpec((tm, LANE), row),
    )
    return pl.pallas_call(
        _in_proj_kernel,
        grid=(m // tm,),
        in_specs=[pl.BlockSpec((tm, D_MODEL), row), _full(w.shape), _full(wvt.shape)],
        out_specs=out_specs,
        out_shape=out_shape,
        compiler_params=_params(("parallel",)),
        name="in_proj",
    )(x2, w, wvt)


def _fox_cum_kernel(fl_ref, bias_ref, c3_ref):
    v = fl_ref[...] + bias_ref[...]
    ls = jnp.minimum(v, 0.0) - jnp.log1p(jnp.exp(-jnp.abs(v)))
    c = _cumsum_rows(ls) * LOG2E
    hi = c.astype(BF16).astype(F32)
    r1 = c - hi
    mid = r1.astype(BF16).astype(F32)
    lo = (r1 - mid).astype(BF16).astype(F32)
    lane = lax.broadcasted_iota(jnp.int32, c.shape, 1)
    out = jnp.where(lane < N_HEADS, hi,
                    jnp.where(lane < 2 * N_HEADS, pltpu.roll(mid, N_HEADS, 1),
                              jnp.where(lane < C3_ONE, pltpu.roll(lo, 2 * N_HEADS, 1),
                                        jnp.where(lane == C3_ONE, 1.0, 0.0))))
    c3_ref[...] = out.astype(BF16)


def _fox_cum(fl, bias_pad, bsz, seq):
    return pl.pallas_call(
        _fox_cum_kernel,
        grid=(bsz,),
        in_specs=[pl.BlockSpec((seq, LANE), lambda b: (b, 0)), _full((1, LANE))],
        out_specs=pl.BlockSpec((seq, LANE), lambda b: (b, 0)),
        out_shape=jax.ShapeDtypeStruct(fl.shape, BF16),
        compiler_params=_params(("parallel",)),
        name="fox_cum",
    )(fl, bias_pad)


def _bias_placement():
    import numpy as np
    pq = np.zeros((LANE, QK_WIDE), np.float32)
    pk = np.zeros((LANE, QK_WIDE), np.float32)
    for h in range(N_HEADS):
        base = h * LANE + AUG0
        for piece in range(3):
            pq[piece * N_HEADS + h, base + piece] = 1.0
            pq[C3_ONE, base + 3 + piece] = 1.0
            pk[C3_ONE, base + piece] = 1.0
            pk[piece * N_HEADS + h, base + 3 + piece] = -1.0
    return jnp.asarray(pq, BF16), jnp.asarray(pk, BF16)


def _fox_attn_kernel(q_ref, k_ref, vt_ref, c3q_ref, c3k_ref, pq_ref, pk_ref, o_ref, ka_ref, sa_ref,
                     sb_ref, *, tq, tk, rows_k):
    qi = pl.program_id(1)
    seq = k_ref.shape[0]

    @pl.when(qi == 0)
    def _():
        def fill(i, _):
            r0 = pl.multiple_of(i * rows_k, rows_k)
            aug = jnp.dot(c3k_ref[pl.ds(r0, rows_k), :], pk_ref[...], preferred_element_type=F32)
            ka_ref[pl.ds(r0, rows_k), :] = (k_ref[pl.ds(r0, rows_k), :].astype(F32)
                                            + aug).astype(BF16)
            return 0
        lax.fori_loop(0, seq // rows_k, fill, 0)

    qa = (q_ref[...].astype(F32)
          + jnp.dot(c3q_ref[...], pq_ref[...], preferred_element_type=F32)).astype(BF16)
    krow = lax.broadcasted_iota(jnp.int32, (tk, tq), 0)
    qcol = lax.broadcasted_iota(jnp.int32, (tk, tq), 1)
    nkb = tq // tk
    nt = (((1,), (1,)), ((), ()))
    ones_rows = jnp.ones((ONES_ROWS, tk), BF16)

    def tree(fn, s):
        parts = [s[i * SUBLANE:(i + 1) * SUBLANE] for i in range(s.shape[0] // SUBLANE)]
        while len(parts) > 1:
            parts = [fn(parts[i], parts[i + 1]) for i in range(0, len(parts), 2)]
        return parts[0]

    hpb = sa_ref.shape[0]
    for j in range(N_HEADS // hpb):
        heads = tuple(range(j * hpb, (j + 1) * hpb))
        qh = tuple(qa[:, h * LANE:(h + 1) * LANE] for h in heads)

        def scores(buf, kb, heads=heads, qh=qh):
            r0 = pl.multiple_of(kb * tk, tk)
            for e, h in enumerate(heads):
                buf[e] = lax.dot_general(ka_ref[pl.ds(r0, tk), h * LANE:(h + 1) * LANE], qh[e], nt,
                                         preferred_element_type=F32)

        def absorb(buf, kb, stats, diag, nxt, heads=heads):
            if nxt is not None:
                scores(*nxt)
            new = []
            for e, h in enumerate(heads):
                m, acc = stats[e]
                se = buf[e]
                if diag is not None:
                    se = jnp.where(krow + diag * tk <= qcol, se, NEG_BIG)
                m_new = jnp.maximum(m, jnp.max(tree(jnp.maximum, se), axis=0, keepdims=True))
                p = jnp.exp2(se - m_new)
                alpha = jnp.exp2(m - m_new)
                vth = jnp.concatenate([vt_ref[kb, h * HEAD_DIM:(h + 1) * HEAD_DIM, :], ones_rows],
                                      axis=0)
                pv = jnp.dot(vth, p.astype(BF16), preferred_element_type=F32)
                new.append((m_new, alpha * acc + pv))
            return tuple(new)

        def step(t, stats):
            kb = nkb * t
            stats = absorb(sa_ref, kb, stats, None, (sb_ref, kb + 1))
            return absorb(sb_ref, kb + 1, stats, None, (sa_ref, kb + 2))

        init1 = (jnp.full((1, tq), NEG_BIG, F32), jnp.zeros((HEAD_DIM + ONES_ROWS, tq), F32))
        scores(sa_ref, 0)
        stats = lax.fori_loop(0, qi, step, (init1,) * hpb)
        kb = nkb * qi
        stats = absorb(sa_ref, kb, stats, 0, (sb_ref, kb + 1))
        stats = absorb(sb_ref, kb + 1, stats, 1, None)
        for e in range(0, hpb, 2):
            outs = []
            for _, acc in (stats[e], stats[e + 1]):
                outs.append(acc[:HEAD_DIM] / acc[HEAD_DIM:HEAD_DIM + 1])
            out_t = jnp.concatenate(outs, axis=0)
            c0 = (heads[e] // 2) * LANE
            o_ref[:, c0:c0 + LANE] = out_t.T.astype(o_ref.dtype)


def _fox_attn(q, k, vt, c3, pq, pk, bsz, seq, tk, hpb=4):
    m = q.shape[0]
    tq = 2 * tk
    nq = seq // tq
    vt4 = vt.reshape(bsz, seq // tk, WIDTH, tk)
    kernel = functools.partial(_fox_attn_kernel, tq=tq, tk=tk, rows_k=min(512, seq))
    return pl.pallas_call(
        kernel,
        grid=(bsz, nq),
        in_specs=[
            pl.BlockSpec((tq, QK_WIDE), lambda b, i: (b * nq + i, 0)),
            pl.BlockSpec((seq, QK_WIDE), lambda b, i: (b, 0)),
            pl.BlockSpec((None, seq // tk, WIDTH, tk), lambda b, i: (b, 0, 0, 0)),
            pl.BlockSpec((tq, LANE), lambda b, i: (b * nq + i, 0)),
            pl.BlockSpec((seq, LANE), lambda b, i: (b, 0)),
            _full(pq.shape), _full(pk.shape),
        ],
        out_specs=pl.BlockSpec((tq, WIDTH), lambda b, i: (b * nq + i, 0)),
        out_shape=jax.ShapeDtypeStruct((m, WIDTH), BF16),
        scratch_shapes=[pltpu.VMEM((seq, QK_WIDE), BF16), pltpu.VMEM((hpb, tk, tq), F32),
                        pltpu.VMEM((hpb, tk, tq), F32)],
        compiler_params=_params(("parallel", "arbitrary")),
        name="fox_attn",
    )(q, k, vt4, c3, c3, pq, pk)


def _ssd_kernel(xbc_ref, halo_ref, z_ref, fl_ref, erep_ref, cw_ref, cb_ref, dtb_ref, alog_ref, d_ref,
                nw_ref, o_ref, state_ref):
    c = pl.program_id(1)
    q = SSD_CHUNK

    @pl.when(c == 0)
    def _():
        state_ref[...] = jnp.zeros_like(state_ref)

    halo = jnp.where(c > 0, halo_ref[...], 0.0)
    full = jnp.concatenate([halo, xbc_ref[...]], axis=0)
    cw = cw_ref[...]
    acc = cb_ref[...]
    for k in range(SSD_CONV):
        off = SUBLANE - (SSD_CONV - 1) + k
        acc = acc + cw[k:k + 1, :] * full[off:off + q, :]
    xc = acc * _sigmoid(acc)
    xs = xc[:, :WIDTH]
    bm = xc[:, WIDTH:WIDTH + LANE]
    cm = xc[:, WIDTH + LANE:]

    dt8 = _softplus(fl_ref[...] + dtb_ref[...])
    hi = dt8.astype(BF16)
    r1 = dt8 - hi.astype(F32)
    mid = r1.astype(BF16)
    low = (r1 - mid.astype(F32)).astype(BF16)
    dt = jnp.dot(jnp.concatenate([hi, mid, low], axis=1), erep_ref[...],
                 preferred_element_type=F32)
    adt = dt * (-jnp.exp(alog_ref[...]))
    a_cs = _cumsum_rows(adt)
    a_tot = a_cs[q - 1:q, :]
    xdt = xs * dt
    xw = (xdt * jnp.exp(a_tot - a_cs)).astype(BF16)
    dec_out = jnp.exp(a_cs)
    dec_tot = jnp.exp(a_tot)
    a_cs_t = a_cs.T
    xdt_b = xdt.astype(BF16)

    lo = lax.broadcasted_iota(jnp.int32, (1, LANE), 1) < SSD_STATE
    tril = (lax.broadcasted_iota(jnp.int32, (q, q), 0)
            >= lax.broadcasted_iota(jnp.int32, (q, q), 1))
    cm_g = (jnp.where(lo, cm, 0.0).astype(BF16), jnp.where(lo, 0.0, cm).astype(BF16))
    bm_g = (jnp.where(lo, bm, 0.0).astype(BF16), jnp.where(lo, 0.0, bm).astype(BF16))
    bm_b = bm.astype(BF16)
    nt = (((1,), (1,)), ((), ()))
    tn = (((0,), (0,)), ((), ()))
    cb = tuple(lax.dot_general(cm_g[g], bm_b, nt, preferred_element_type=F32) for g in (0, 1))

    ys = []
    for j in range(N_HEADS // 2):
        g = j // (N_HEADS // 2 // SSD_GROUPS)
        c0, c1 = j * LANE, (j + 1) * LANE
        yd = []
        for e in (0, 1):
            hl = (2 * j + e) * HEAD_DIM
            diff = a_cs[:, hl:hl + 1] - a_cs_t[hl:hl + 1, :]
            lm = jnp.exp(jnp.where(tril, diff, NEG_BIG))
            yd.append(jnp.dot((cb[g] * lm).astype(BF16), xdt_b[:, c0:c1],
                              preferred_element_type=F32))
        y_diag = jnp.where(lo, yd[0], yd[1])
        st = state_ref[j]
        y_off = jnp.dot(cm_g[g], st.astype(BF16), preferred_element_type=F32) * dec_out[:, c0:c1]
        state_ref[j] = dec_tot[:, c0:c1] * st + lax.dot_general(
            bm_g[g], xw[:, c0:c1], tn, preferred_element_type=F32)
        ys.append(y_diag + y_off)
    y = jnp.concatenate(ys, axis=1) + d_ref[...] * xs
    zf = z_ref[...]
    y = y * (zf * _sigmoid(zf))
    gw = WIDTH // SSD_GROUPS
    outs = []
    for g in range(SSD_GROUPS):
        yg = y[:, g * gw:(g + 1) * gw]
        outs.append(yg * lax.rsqrt(jnp.mean(yg * yg, axis=-1, keepdims=True) + RMS_EPS))
    o_ref[...] = (jnp.concatenate(outs, axis=1) * nw_ref[...]).astype(o_ref.dtype)


def _head_repeat_matrix():
    import numpy as np
    e = np.zeros((LANE, WIDTH), np.float32)
    for h in range(N_HEADS):
        e[N_HEADS + h, h * HEAD_DIM:(h + 1) * HEAD_DIM] = 1.0
    return jnp.asarray(np.concatenate([e, e, e], axis=0), BF16)


def _ssd(xbc, z, fl, erep, cw, cb, dtb_rep, alog_rep, d_rep, nw, bsz, seq):
    m = xbc.shape[0]
    q = SSD_CHUNK
    nc = seq // q
    row = lambda b, c: (b * nc + c, 0)
    halo = lambda b, c: (jnp.maximum((b * nc + c) * (q // SUBLANE) - 1, 0), 0)
    return pl.pallas_call(
        _ssd_kernel,
        grid=(bsz, nc),
        in_specs=[
            pl.BlockSpec((q, SSD_CONV_DIM), row),
            pl.BlockSpec((SUBLANE, SSD_CONV_DIM), halo),
            pl.BlockSpec((q, WIDTH), row),
            pl.BlockSpec((q, LANE), row),
            _full(erep.shape),
            _full(cw.shape), _full(cb.shape), _full(dtb_rep.shape), _full(alog_rep.shape),
            _full(d_rep.shape), _full(nw.shape),
        ],
        out_specs=pl.BlockSpec((q, WIDTH), row),
        out_shape=jax.ShapeDtypeStruct((m, WIDTH), BF16),
        scratch_shapes=[pltpu.VMEM((N_HEADS // 2, LANE, LANE), F32)],
        compiler_params=_params(("parallel", "arbitrary")),
        name="ssd",
    )(xbc, xbc, z, fl, erep, cw, cb, dtb_rep, alog_rep, d_rep, nw)


def _s5_kernel(u_ref, wb_ref, wc_ref, are_ref, aim_ref, d_ref, wg_ref, bg_ref, o_ref,
               bu_ref, h_ref, *, steps, bsz):
    n = S5_LANES

    @pl.when(pl.program_id(0) == 0)
    def _():
        h_ref[...] = jnp.zeros_like(h_ref)

    u = pltpu.einshape("bth->tbh", u_ref[...]).reshape(steps * bsz, WIDTH)
    ub = u.astype(BF16)
    nslab = S5_GROUPS // S5_SLAB_GROUPS
    sw, sn = S5_SLAB_GROUPS * S5_GROUP, S5_SLAB_GROUPS * S5_STATE
    for c in range(nslab):
        for part in range(2):
            c0 = part * n + c * sn
            bu_ref[:, c0:c0 + sn] = jnp.dot(ub[:, c * sw:(c + 1) * sw],
                                            wb_ref[c * sw:(c + 1) * sw, c0:c0 + sn],
                                            preferred_element_type=F32)
    ar = are_ref[...]
    ai = aim_ref[...]

    def step(t, carry):
        hr, hi = carry
        r0 = pl.multiple_of(t * bsz, bsz)
        br = bu_ref[pl.ds(r0, bsz), 0:n]
        bi = bu_ref[pl.ds(r0, bsz), n:2 * n]
        nr = ar * hr - ai * hi + br
        ni = ar * hi + ai * hr + bi
        bu_ref[pl.ds(r0, bsz), 0:n] = nr
        bu_ref[pl.ds(r0, bsz), n:2 * n] = ni
        return nr, ni

    hr, hi = lax.fori_loop(0, steps, step, (h_ref[:, 0:n], h_ref[:, n:2 * n]))
    h_ref[:, 0:n] = hr
    h_ref[:, n:2 * n] = hi
    ys = []
    for c in range(nslab):
        yc = None
        for part in range(2):
            c0 = part * n + c * sn
            t = jnp.dot(bu_ref[:, c0:c0 + sn].astype(BF16), wc_ref[c0:c0 + sn, c * sw:(c + 1) * sw],
                        preferred_element_type=F32)
            yc = t if yc is None else yc + t
        ys.append(yc)
    y = jnp.concatenate(ys, axis=1) + d_ref[...] * u
    gy = jax.nn.gelu(y)
    gl = jnp.dot(gy.astype(BF16), wg_ref[...], preferred_element_type=F32) + bg_ref[...]
    out = (gy * _sigmoid(gl)).reshape(steps, bsz, WIDTH)
    o_ref[...] = pltpu.einshape("tbh->bth", out).astype(o_ref.dtype)


def _s5(u3, wb, wc, a_re, a_im, d, wg, bg, bsz, seq, steps=128):
    rows = steps * bsz
    kernel = functools.partial(_s5_kernel, steps=steps, bsz=bsz)
    return pl.pallas_call(
        kernel,
        grid=(seq // steps,),
        in_specs=[
            pl.BlockSpec((bsz, steps, WIDTH), lambda i: (0, i, 0)),
            _full(wb.shape), _full(wc.shape), _full(a_re.shape), _full(a_im.shape),
            _full(d.shape), _full(wg.shape), _full(bg.shape),
        ],
        out_specs=pl.BlockSpec((bsz, steps, WIDTH), lambda i: (0, i, 0)),
        out_shape=jax.ShapeDtypeStruct(u3.shape, BF16),
        scratch_shapes=[pltpu.VMEM((rows, 2 * S5_LANES), F32),
                        pltpu.VMEM((bsz, 2 * S5_LANES), F32)],
        compiler_params=_params(("arbitrary",)),
        name="s5",
    )(u3, wb, wc, a_re, a_im, d, wg, bg)


def _s5_weights(a_re, a_im, b_re, b_im, c_re, c_im, log_step, bsz):
    lam_re = jnp.minimum(a_re, -1e-4)
    lam_im = a_im
    step = jnp.exp(log_step)[:, None]
    mag = jnp.exp(lam_re * step)
    abar_re = mag * jnp.cos(lam_im * step)
    abar_im = mag * jnp.sin(lam_im * step)
    den = lam_re * lam_re + lam_im * lam_im
    num_re = abar_re - 1.0
    k_re = (num_re * lam_re + abar_im * lam_im) / den
    k_im = (abar_im * lam_re - num_re * lam_im) / den
    bb_re = k_re[..., None] * b_re - k_im[..., None] * b_im
    bb_im = k_re[..., None] * b_im + k_im[..., None] * b_re
    eye = jnp.eye(S5_GROUPS, dtype=F32)

    def in_blocks(bb):
        t = jnp.transpose(bb, (0, 2, 1))[:, :, None, :] * eye[:, None, :, None]
        return t.reshape(WIDTH, S5_LANES)

    def out_blocks(cc):
        t = jnp.transpose(cc, (0, 2, 1))[:, :, None, :] * eye[:, None, :, None]
        return t.reshape(S5_LANES, WIDTH)

    wb = jnp.concatenate([in_blocks(bb_re), in_blocks(bb_im)], axis=1).astype(BF16)
    wc = jnp.concatenate([out_blocks(c_re), -out_blocks(c_im)], axis=0).astype(BF16)
    rep = lambda a: jnp.broadcast_to(a.reshape(1, S5_LANES), (bsz, S5_LANES))
    return wb, wc, rep(abar_re), rep(abar_im)


def _merge_kernel(ya_ref, yb_ref, yc_ref, gate_ref, x_ref, wbr_ref, bg_ref, wo_ref, g_ref, b_ref,
                  o_ref, *, alpha):
    tm = x_ref.shape[0]
    half = tm // 2
    spans = ((0, half), (half, tm))
    prs = [[jnp.dot(y_ref[a:b, :], wbr_ref[r], preferred_element_type=F32)
            for r, y_ref in enumerate((ya_ref, yb_ref, yc_ref))] for a, b in spans]
    for (a, b), pr in zip(spans, prs):
        merged = None
        for r in range(N_BRANCH):
            gt = _sigmoid(gate_ref[a:b, r * D_MODEL:(r + 1) * D_MODEL] + bg_ref[r:r + 1, :])
            merged = gt * pr[r] if merged is None else merged + gt * pr[r]
        mix = jnp.dot(merged.astype(BF16), wo_ref[...], preferred_element_type=F32)
        o_ref[a:b, :] = _layer_norm(alpha * x_ref[a:b, :] + mix, g_ref[...], b_ref[...])


def _merge(ya, yb, yc, gates, x2, wbr, bg, wo, g, b, alpha, tm=1024):
    m = x2.shape[0]
    row = lambda i: (i, 0)
    kernel = functools.partial(_merge_kernel, alpha=alpha)
    return pl.pallas_call(
        kernel,
        grid=(m // tm,),
        in_specs=[
            pl.BlockSpec((tm, WIDTH), row),
            pl.BlockSpec((tm, WIDTH), row),
            pl.BlockSpec((tm, WIDTH), row),
            pl.BlockSpec((tm, N_BRANCH * D_MODEL), row),
            pl.BlockSpec((tm, D_MODEL), row),
            _full(wbr.shape), _full(bg.shape), _full(wo.shape), _full(g.shape), _full(b.shape),
        ],
        out_specs=pl.BlockSpec((tm, D_MODEL), row),
        out_shape=jax.ShapeDtypeStruct((m, D_MODEL), F32),
        compiler_params=_params(("parallel",)),
        name="merge_ln1",
    )(ya, yb, yc, gates, x2, wbr, bg, wo, g, b)


def _ffn_kernel(x_ref, halo_ref, wu_ref, cw_ref, cb_ref, wd_ref, g_ref, b_ref, o_ref, *,
                alpha, nt, tf, rb):
    tm = x_ref.shape[0]
    first = (pl.program_id(0) % nt) == 0
    x = x_ref[...]
    halo = jnp.where(first, 0.0, halo_ref[...])
    xf = jnp.concatenate([halo, x], axis=0)
    nrb = tm // rb
    xs = [xf[(0 if r == 0 else SUBLANE + r * rb):SUBLANE + (r + 1) * rb].astype(BF16)
          for r in range(nrb)]

    def up(j, r):
        cv, cg = j * tf, D_FF + j * tf
        return (jnp.dot(xs[r], wu_ref[:, cv:cv + tf], preferred_element_type=F32),
                jnp.dot(xs[r], wu_ref[:, cg:cg + tf], preferred_element_type=F32))

    def conv(h, c0, r):
        acc = cb_ref[:, c0:c0 + tf]
        for k in range(FFN_CONV):
            off = SUBLANE - (FFN_CONV - 1) + k + r * rb
            acc = acc + cw_ref[k:k + 1, c0:c0 + tf] * h[off:off + rb, :]
        return acc

    nch = D_FF // tf
    outs = [None] * nrb
    pieces = [up(0, r) for r in range(nrb)]
    for j in range(nch):
        cv, cg = j * tf, D_FF + j * tf
        hv = jnp.concatenate([p[0] for p in pieces], axis=0)
        hg = jnp.concatenate([p[1] for p in pieces], axis=0)
        nxt = []
        for r in range(nrb):
            if j + 1 < nch:
                nxt.append(up(j + 1, r))
            val = conv(hv, cv, r)
            gate = conv(hg, cg, r)
            act = (gate * _sigmoid(gate) * val).astype(BF16)
            part = jnp.dot(act, wd_ref[cv:cv + tf, :], preferred_element_type=F32)
            outs[r] = part if outs[r] is None else outs[r] + part
        pieces = nxt
    out = jnp.concatenate(outs, axis=0)
    o_ref[...] = _layer_norm(alpha * x + out, g_ref[...], b_ref[...])


def _ffn(x2, wu, cw, cb, wd, g, b, seq, alpha, tm=1024, tf=256, rb=256):
    m = x2.shape[0]
    nt = seq // tm
    row = lambda i: (i, 0)
    kernel = functools.partial(_ffn_kernel, alpha=alpha, nt=nt, tf=tf, rb=rb)
    return pl.pallas_call(
        kernel,
        grid=(m // tm,),
        in_specs=[
            pl.BlockSpec((tm, D_MODEL), row),
            pl.BlockSpec((SUBLANE, D_MODEL), lambda i: (jnp.maximum(i * (tm // SUBLANE) - 1, 0), 0)),
            _full(wu.shape), _full(cw.shape), _full(cb.shape), _full(wd.shape),
            _full(g.shape), _full(b.shape),
        ],
        out_specs=pl.BlockSpec((tm, D_MODEL), row),
        out_shape=jax.ShapeDtypeStruct((m, D_MODEL), F32),
        compiler_params=_params(("parallel",)),
        name="ffn_ln2",
    )(x2, x2, wu, cw, cb, wd, g, b)


def _in_proj_weight(w_in):
    o = 0
    parts = {}
    for name, size in (("q", WIDTH), ("k", WIDTH), ("v", WIDTH), ("f", N_HEADS), ("z", WIDTH),
                       ("xbc", SSD_CONV_DIM), ("dt", N_HEADS), ("u", WIDTH),
                       ("gate", N_BRANCH * D_MODEL)):
        parts[name] = w_in[:, o:o + size]
        o += size
    small = jnp.pad(jnp.concatenate([parts["f"], parts["dt"]], axis=1),
                    ((0, 0), (0, LANE - 2 * N_HEADS)))
    q_scale = HEAD_DIM ** -0.5 * LOG2E
    w = jnp.concatenate([parts["q"] * q_scale, parts["k"], parts["xbc"], parts["z"],
                         parts["u"], parts["gate"], small], axis=1)
    return w.astype(BF16), parts["v"].T.astype(BF16)


def kernel(x, w_in, fox_f_bias, ssd_conv_w, ssd_conv_b, ssd_dt_bias, ssd_a_log, ssd_d, ssd_norm_w, s5_a_re, s5_a_im, s5_b_re, s5_b_im, s5_c_re, s5_c_im, s5_log_step, s5_d, s5_w_glu, s5_b_glu, w_branch, b_gate, w_out, ln1_g, ln1_b, ffn_w_up, ffn_conv_w, ffn_conv_b, ffn_w_down, ln2_g, ln2_b):
    bsz, seq, _ = x.shape
    depth = w_in.shape[0]
    alpha = (2 * depth) ** 0.25
    tk = 256
    pq, pk = _bias_placement()
    erep = _head_repeat_matrix()
    rep = lambda a: jnp.repeat(a, HEAD_DIM).reshape(1, WIDTH)
    row = lambda a: a.reshape(1, -1)
    x2 = x.reshape(bsz * seq, D_MODEL)
    for i in range(depth):
        w, wvt = _in_proj_weight(w_in[i])
        q, k, vt, xbc, z, u, gates, fl = _in_proj(x2, w, wvt, tk)
        f_bias = jnp.pad(fox_f_bias[i], (0, LANE - N_HEADS)).reshape(1, LANE)
        c3 = _fox_cum(fl, f_bias, bsz, seq)
        ya = _fox_attn(q, k, vt, c3, pq, pk, bsz, seq, tk)
        dt_bias = jnp.pad(ssd_dt_bias[i], (N_HEADS, LANE - 2 * N_HEADS)).reshape(1, LANE)
        yb = _ssd(xbc, z, fl, erep, ssd_conv_w[i], row(ssd_conv_b[i]), dt_bias,
                  rep(ssd_a_log[i]), rep(ssd_d[i]), row(ssd_norm_w[i]), bsz, seq)
        wb, wc, a_re, a_im = _s5_weights(s5_a_re[i], s5_a_im[i], s5_b_re[i], s5_b_im[i],
                                         s5_c_re[i], s5_c_im[i], s5_log_step[i], bsz)
        yc = _s5(u.reshape(bsz, seq, WIDTH), wb, wc, a_re, a_im, row(s5_d[i]),
                 s5_w_glu[i].astype(BF16), row(s5_b_glu[i]), bsz, seq)
        x2 = _merge(ya, yb, yc.reshape(bsz * seq, WIDTH), gates, x2, w_branch[i].astype(BF16),
                    b_gate[i], w_out[i].astype(BF16), row(ln1_g[i]), row(ln1_b[i]), alpha)
        x2 = _ffn(x2, ffn_w_up[i].astype(BF16), ffn_conv_w[i], row(ffn_conv_b[i]),
                  ffn_w_down[i].astype(BF16), row(ln2_g[i]), row(ln2_b[i]), seq, alpha)
    return x2.reshape(bsz, seq, D_MODEL)
```
